```python
import math
import jax, jax.numpy as jnp
from jax import lax
import numpy as np

D_MODEL = 1024
BATCH = 16
SEQ = 4096
DEPTH = 4

N_MIXERS = 3
D_FF = 4 * D_MODEL
DEEPNORM_ALPHA = (2.0 * DEPTH) ** 0.25
DEEPNORM_BETA = (8.0 * DEPTH) ** -0.25
LN_EPS = 1e-5

A_CHUNK = 128
A_WIDTH = D_MODEL
A_GROUPS = 8
A_GROUP_DIM = A_WIDTH // A_GROUPS

B_GROUPS = 4
B_GROUP_DIM = D_MODEL // B_GROUPS

POOL_WINDOWS = (2, 4, 8, 16)
C_GROUPS = len(POOL_WINDOWS)
C_GROUP_DIM = D_MODEL // C_GROUPS

N_A = (DEPTH + 2) // 3
N_B = (DEPTH + 1) // 3
N_C = DEPTH // 3

kernel_name = "hybrid_gmlp_fnet_pool_deepnorm_encoder"


def _layer_norm(x, g, b):
    xf = x.astype(jnp.float32)
    mu = jnp.mean(xf, axis=-1, keepdims=True)
    var = jnp.mean(jnp.square(xf - mu), axis=-1, keepdims=True)
    return ((xf - mu) * lax.rsqrt(var + LN_EPS) * g + b).astype(x.dtype)


def _centred_window_mean(z, w):
    b, s, c = z.shape
    zf = z.astype(jnp.float32)
    cs = jnp.concatenate([jnp.zeros((b, 1, c), jnp.float32), jnp.cumsum(zf, axis=1)], axis=1)
    t = jnp.arange(s)
    lo = jnp.clip(t - w // 2, 0, s)
    hi = jnp.clip(t - w // 2 + w, 0, s)
    window_sum = jnp.take(cs, hi, axis=1) - jnp.take(cs, lo, axis=1)
    count = (hi - lo).astype(jnp.float32)[None, :, None]
    return (window_sum / count).astype(z.dtype)


def spatial_gating_mixer(x, w_in, ln_g, ln_b, w_s, b_s, w_out):
    bsz, s, _ = x.shape
    z = jax.nn.gelu(x @ w_in)
    u, v = jnp.split(z, 2, axis=-1)
    v = _layer_norm(v, ln_g, ln_b)
    v = v.reshape(bsz, s // A_CHUNK, A_CHUNK, A_GROUPS, A_GROUP_DIM)
    mixed = jnp.einsum('gqp,bnpgd->bnqgd', w_s, v) + b_s.T[None, None, :, :, None]
    out = u * mixed.reshape(bsz, s, A_WIDTH)
    return out @ w_out


def fourier_mixer(x, w_in, ln_g, ln_b, w_out):
    bsz, s, _ = x.shape
    z = (x @ w_in).reshape(bsz, s, B_GROUPS, B_GROUP_DIM)
    z = _layer_norm(z, ln_g, ln_b)
    f = jnp.fft.fft2(z.astype(jnp.float32), axes=(1, 3), norm="ortho").real
    return f.astype(x.dtype).reshape(bsz, s, D_MODEL) @ w_out


def multiscale_pool_mixer(x, w_in, w_grp, scale, w_out):
    bsz, s, _ = x.shape
    z = (x @ w_in).reshape(bsz, s, C_GROUPS, C_GROUP_DIM)
    pooled = jnp.stack(
        [_centred_window_mean(z[:, :, g], w) - z[:, :, g] for g, w in enumerate(POOL_WINDOWS)],
        axis=2)
    mixed = jnp.einsum('bsgc,gcd->bsgd', pooled, w_grp).reshape(bsz, s, D_MODEL) * scale
    return mixed @ w_out


def squared_relu_mlp(x, w1, b1, w2, b2):
    h = jnp.square(jax.nn.relu(x @ w1 + b1))
    return h @ w2 + b2


def setup_inputs(seed: int = 0) -> dict:
    key = jax.random.key(seed)
    ks = iter(jax.random.split(key, 32))

    def normal(shape, scale):
        return jax.random.normal(next(ks), shape, jnp.float32) * scale

    def gain(shape):
        return 1.0 + normal(shape, 0.05)

    d, f = D_MODEL, D_FF
    return {
        "x": normal((BATCH, SEQ, d), 1.0),
        "ln1_g": gain((DEPTH, d)),
        "ln1_b": normal((DEPTH, d), 0.02),
        "ffn_w1": normal((DEPTH, d, f), d ** -0.5),
        "ffn_b1": normal((DEPTH, f), 0.02),
        "ffn_w2": normal((DEPTH, f, d), DEEPNORM_BETA * f ** -0.5),
        "ffn_b2": normal((DEPTH, d), 0.02),
        "ln2_g": gain((DEPTH, d)),
        "ln2_b": normal((DEPTH, d), 0.02),
        "a_w_in": normal((N_A, d, 2 * A_WIDTH), d ** -0.5),
        "a_ln_g": gain((N_A, A_WIDTH)),
        "a_ln_b": normal((N_A, A_WIDTH), 0.02),
        "a_w_s": normal((N_A, A_GROUPS, A_CHUNK, A_CHUNK), A_CHUNK ** -0.5),
        "a_b_s": gain((N_A, A_GROUPS, A_CHUNK)),
        "a_w_out": normal((N_A, A_WIDTH, d), DEEPNORM_BETA * A_WIDTH ** -0.5),
        "b_w_in": normal((N_B, d, d), d ** -0.5),
        "b_ln_g": gain((N_B, B_GROUPS, B_GROUP_DIM)),
        "b_ln_b": normal((N_B, B_GROUPS, B_GROUP_DIM), 0.02),
        "b_w_out": normal((N_B, d, d), DEEPNORM_BETA * d ** -0.5),
        "c_w_in": normal((N_C, d, d), d ** -0.5),
        "c_w_grp": normal((N_C, C_GROUPS, C_GROUP_DIM, C_GROUP_DIM), C_GROUP_DIM ** -0.5),
        "c_scale": gain((N_C, d)),
        "c_w_out": normal((N_C, d, d), DEEPNORM_BETA * d ** -0.5),
    }


def reference(x, ln1_g, ln1_b, ffn_w1, ffn_b1, ffn_w2, ffn_b2, ln2_g, ln2_b,
              a_w_in, a_ln_g, a_ln_b, a_w_s, a_b_s, a_w_out,
              b_w_in, b_ln_g, b_ln_b, b_w_out,
              c_w_in, c_w_grp, c_scale, c_w_out):
    for i in range(DEPTH):
        kind, j = i % N_MIXERS, i // N_MIXERS
        if kind == 0:
            y = spatial_gating_mixer(x, a_w_in[j], a_ln_g[j], a_ln_b[j], a_w_s[j], a_b_s[j], a_w_out[j])
        elif kind == 1:
            y = fourier_mixer(x, b_w_in[j], b_ln_g[j], b_ln_b[j], b_w_out[j])
        else:
            y = multiscale_pool_mixer(x, c_w_in[j], c_w_grp[j], c_scale[j], c_w_out[j])
        x = _layer_norm(DEEPNORM_ALPHA * x + y, ln1_g[i], ln1_b[i])
        y = squared_relu_mlp(x, ffn_w1[i], ffn_b1[i], ffn_w2[i], ffn_b2[i])
        x = _layer_norm(DEEPNORM_ALPHA * x + y, ln2_g[i], ln2_b[i])
    return x
```

```python
import functools
import math

import jax
import jax.numpy as jnp
import numpy as np
from jax import lax
from jax.experimental import pallas as pl
from jax.experimental.pallas import tpu as pltpu

DEPTH = 4
N_MIXERS = 3
DEEPNORM_ALPHA = (2.0 * DEPTH) ** 0.25
LN_EPS = 1e-5

A_CHUNK = 128
A_GROUPS = 8
B_GROUPS = 4
POOL_WINDOWS = (2, 4, 8, 16)
POOL_HALO = 8

V7X_VMEM_LIMIT_BYTES = 56 * 1024 * 1024
TOKEN_TILE = 512
FFN_CHUNK = 1024
DFT_TILE = 256

BF16 = jnp.bfloat16
F32 = jnp.float32


def _ln(r, g, b):
    mu = jnp.mean(r, axis=-1, keepdims=True)
    c = r - mu
    var = jnp.mean(c * c, axis=-1, keepdims=True)
    return c * lax.rsqrt(var + LN_EPS) * g + b


def _gelu_tanh(x):
    inner = math.sqrt(2.0 / math.pi) * (x + 0.044715 * (x * x * x))
    return 0.5 * x * (1.0 + jnp.tanh(inner))


def _dot(a, b):
    return jnp.dot(a, b, preferred_element_type=F32)


def _const_spec(shape):
    zeros = (0,) * len(shape)
    return pl.BlockSpec(shape, lambda *_: zeros, pipeline_mode=pl.Buffered(1))


def _params(n_axes):
    return pltpu.CompilerParams(
        dimension_semantics=("arbitrary",) * n_axes,
        vmem_limit_bytes=V7X_VMEM_LIMIT_BYTES)


def _ffn_kernel(x_ref, w1_ref, b1_ref, w2_ref, b2_ref, g_ref, b_ref, o_ref):
    x = x_ref[...]
    xb = x.astype(BF16)
    d_ff = w1_ref.shape[1]
    acc = jnp.zeros(x.shape, F32)
    for c in range(d_ff // FFN_CHUNK):
        cols = slice(c * FFN_CHUNK, (c + 1) * FFN_CHUNK)
        h = _dot(xb, w1_ref[:, cols]) + b1_ref[:, cols]
        h = jnp.square(jnp.maximum(h, 0.0))
        acc = acc + _dot(h.astype(BF16), w2_ref[cols, :])
    y = acc + b2_ref[...]
    o_ref[...] = _ln(DEEPNORM_ALPHA * x + y, g_ref[...], b_ref[...])


def _ffn(x2d, w1, b1, w2, b2, g, b):
    n, d = x2d.shape
    f = w1.shape[1]
    row = pl.BlockSpec((TOKEN_TILE, d), lambda i: (i, 0))
    return pl.pallas_call(
        _ffn_kernel,
        grid=(n // TOKEN_TILE,),
        in_specs=[row, _const_spec((d, f)), _const_spec((1, f)),
                  _const_spec((f, d)), _const_spec((1, d)),
                  _const_spec((1, d)), _const_spec((1, d))],
        out_specs=row,
        out_shape=jax.ShapeDtypeStruct((n, d), F32),
        compiler_params=_params(1),
        name="ffn",
    )(x2d, w1.astype(BF16), b1.reshape(1, f), w2.astype(BF16),
      b2.reshape(1, d), g.reshape(1, d), b.reshape(1, d))


def _gmlp_kernel(x_ref, w_in_ref, lng_ref, lnb_ref, ws_ref, bs_ref, w_out_ref,
                 g_ref, b_ref, o_ref):
    x = x_ref[...]
    width = w_out_ref.shape[0]
    gd = width // A_GROUPS
    n_chunks = x.shape[0] // A_CHUNK
    z = _gelu_tanh(_dot(x.astype(BF16), w_in_ref[...]))
    u = z[:, :width]
    v = _ln(z[:, width:], lng_ref[...], lnb_ref[...]).astype(BF16)
    mixed_groups = []
    for gi in range(A_GROUPS):
        cols = slice(gi * gd, (gi + 1) * gd)
        v_g = jnp.concatenate(
            [v[n * A_CHUNK:(n + 1) * A_CHUNK, cols] for n in range(n_chunks)],
            axis=1)
        mixed_groups.append(_dot(ws_ref[gi], v_g))
    rows = []
    for n in range(n_chunks):
        rows.append(jnp.concatenate(
            [m[:, n * gd:(n + 1) * gd] for m in mixed_groups], axis=1))
    mixed = jnp.concatenate(rows, axis=0)
    bias = jnp.concatenate([bs_ref[...]] * n_chunks, axis=0)
    out = u * (mixed + bias)
    y = _dot(out.astype(BF16), w_out_ref[...])
    o_ref[...] = _ln(DEEPNORM_ALPHA * x + y, g_ref[...], b_ref[...])


def _gmlp_mixer(x2d, w_in, ln_g, ln_b, w_s, b_s, w_out, g, b):
    n, d = x2d.shape
    width = w_out.shape[0]
    gd = width // A_GROUPS
    bias_map = jnp.repeat(b_s.T, gd, axis=1)
    row = pl.BlockSpec((TOKEN_TILE, d), lambda i: (i, 0))
    return pl.pallas_call(
        _gmlp_kernel,
        grid=(n // TOKEN_TILE,),
        in_specs=[row, _const_spec((d, 2 * width)), _const_spec((1, width)),
                  _const_spec((1, width)),
                  _const_spec((A_GROUPS, A_CHUNK, A_CHUNK)),
                  _const_spec((A_CHUNK, width)), _const_spec((width, d)),
                  _const_spec((1, d)), _const_spec((1, d))],
        out_specs=row,
        out_shape=jax.ShapeDtypeStruct((n, d), F32),
        compiler_params=_params(1),
        name="gmlp_mixer",
    )(x2d, w_in.astype(BF16), ln_g.reshape(1, width), ln_b.reshape(1, width),
      w_s.astype(BF16), bias_map, w_out.astype(BF16),
      g.reshape(1, d), b.reshape(1, d))


def _fnet_in_kernel(x_ref, w_in_ref, lng_ref, lnb_ref, z_ref):
    z = _dot(x_ref[...].astype(BF16), w_in_ref[...])
    gd = z.shape[1] // B_GROUPS
    for gi in range(B_GROUPS):
        cols = slice(gi * gd, (gi + 1) * gd)
        z_ref[:, cols] = _ln(z[:, cols], lng_ref[:, cols],
                             lnb_ref[:, cols]).astype(BF16)


def _fnet_in(x2d, w_in, ln_g, ln_b):
    n, d = x2d.shape
    row = pl.BlockSpec((TOKEN_TILE, d), lambda i: (i, 0))
    return pl.pallas_call(
        _fnet_in_kernel,
        grid=(n // TOKEN_TILE,),
        in_specs=[row, _const_spec((d, d)), _const_spec((1, d)),
                  _const_spec((1, d))],
        out_specs=row,
        out_shape=jax.ShapeDtypeStruct((n, d), BF16),
        compiler_params=_params(1),
        name="fnet_in",
    )(x2d, w_in.astype(BF16), ln_g.reshape(1, d), ln_b.reshape(1, d))


def _fnet_dft_kernel(x_ref, z_ref, cs_ref, ss_ref, cdft_ref, w_out_ref,
                     g_ref, b_ref, o_ref):
    x = x_ref[0]
    z = z_ref[0]
    p = _dot(cs_ref[...], z).astype(BF16)
    q = _dot(ss_ref[...], z).astype(BF16)
    gd = z.shape[1] // B_GROUPS
    f_groups = []
    for gi in range(B_GROUPS):
        cols = slice(gi * gd, (gi + 1) * gd)
        pq = jnp.concatenate([p[:, cols], q[:, cols]], axis=1)
        f_groups.append(_dot(pq, cdft_ref[...]))
    f = jnp.concatenate(f_groups, axis=1)
    y = _dot(f.astype(BF16), w_out_ref[...])
    o_ref[0] = _ln(DEEPNORM_ALPHA * x + y, g_ref[...], b_ref[...])


def _dft_tables(n):
    j = lax.broadcasted_iota(jnp.int32, (n, n), 0)
    k = lax.broadcasted_iota(jnp.int32, (n, n), 1)
    ang = ((j * k) % n).astype(F32) * (2.0 * math.pi / n)
    return jnp.cos(ang), jnp.sin(ang)


def _fnet_dft(x, z, w_out, g, b):
    bsz, s, d = x.shape
    gd = d // B_GROUPS
    cos_s, sin_s = _dft_tables(s)
    cos_c, sin_c = _dft_tables(gd)
    norm = 1.0 / math.sqrt(s * gd)
    cdft = (jnp.concatenate([cos_c, -sin_c], axis=0) * norm).astype(BF16)
    tile = pl.BlockSpec((1, DFT_TILE, d), lambda bi, ki: (bi, ki, 0))
    seq = pl.BlockSpec((1, s, d), lambda bi, ki: (bi, 0, 0))
    dft_rows = pl.BlockSpec((DFT_TILE, s), lambda bi, ki: (ki, 0))
    return pl.pallas_call(
        _fnet_dft_kernel,
        grid=(bsz, s // DFT_TILE),
        in_specs=[tile, seq, dft_rows, dft_rows, _const_spec((2 * gd, gd)),
                  _const_spec((d, d)), _const_spec((1, d)), _const_spec((1, d))],
        out_specs=tile,
        out_shape=jax.ShapeDtypeStruct((bsz, s, d), F32),
        compiler_params=_params(2),
        name="fnet_dft",
    )(x, z, cos_s.astype(BF16), sin_s.astype(BF16), cdft, w_out.astype(BF16),
      g.reshape(1, d), b.reshape(1, d))


def _fnet_mixer(x, w_in, ln_g, ln_b, w_out, g, b):
    bsz, s, d = x.shape
    z = _fnet_in(x.reshape(bsz * s, d), w_in, ln_g.reshape(d), ln_b.reshape(d))
    return _fnet_dft(x, z.reshape(bsz, s, d), w_out, g, b)


def _pool_kernel(x_ref, prev_ref, next_ref, w_in_ref, w_grp_ref, scale_ref,
                 w_out_ref, g_ref, b_ref, o_ref, z_scr, *, seq_len):
    ti = pl.program_id(1)
    tm = x_ref.shape[1]
    x = x_ref[0]
    w_in = w_in_ref[...]
    first = ti == 0
    last = ti == pl.num_programs(1) - 1
    z_prev = _dot(prev_ref[0].astype(BF16), w_in)
    z_next = _dot(next_ref[0].astype(BF16), w_in)
    z_scr[0:POOL_HALO, :] = jnp.where(first, 0.0, z_prev)
    z_scr[POOL_HALO:POOL_HALO + tm, :] = _dot(x.astype(BF16), w_in)
    z_scr[POOL_HALO + tm:, :] = jnp.where(last, 0.0, z_next)

    t = ti * tm + lax.broadcasted_iota(jnp.int32, (tm, 1), 0)
    gd = x.shape[1] // len(POOL_WINDOWS)
    mixed_groups = []
    for gi, w in enumerate(POOL_WINDOWS):
        cols = slice(gi * gd, (gi + 1) * gd)
        half = w // 2
        win = z_scr[POOL_HALO - half:POOL_HALO - half + tm, cols]
        for off in range(1 - half, half):
            win = win + z_scr[POOL_HALO + off:POOL_HALO + off + tm, cols]
        lo = jnp.maximum(t - half, 0)
        hi = jnp.minimum(t - half + w, seq_len)
        count = (hi - lo).astype(F32)
        pooled = win / count - z_scr[POOL_HALO:POOL_HALO + tm, cols]
        mixed_groups.append(_dot(pooled.astype(BF16), w_grp_ref[gi]))
    mixed = jnp.concatenate(mixed_groups, axis=1) * scale_ref[...]
    y = _dot(mixed.astype(BF16), w_out_ref[...])
    o_ref[0] = _ln(DEEPNORM_ALPHA * x + y, g_ref[...], b_ref[...])


def _pool_mixer(x, w_in, w_grp, scale, w_out, g, b):
    bsz, s, d = x.shape
    n_g = len(POOL_WINDOWS)
    gd = d // n_g
    tm = TOKEN_TILE
    per_tile = tm // POOL_HALO
    n_halo = s // POOL_HALO
    tile = pl.BlockSpec((1, tm, d), lambda bi, ti: (bi, ti, 0))
    prev = pl.BlockSpec(
        (1, POOL_HALO, d),
        lambda bi, ti: (bi, jnp.maximum(ti * per_tile - 1, 0), 0))
    nxt = pl.BlockSpec(
        (1, POOL_HALO, d),
        lambda bi, ti: (bi, jnp.minimum((ti + 1) * per_tile, n_halo - 1), 0))
    return pl.pallas_call(
        functools.partial(_pool_kernel, seq_len=s),
        grid=(bsz, s // tm),
        in_specs=[tile, prev, nxt, _const_spec((d, d)),
                  _const_spec((n_g, gd, gd)), _const_spec((1, d)),
                  _const_spec((d, d)), _const_spec((1, d)), _const_spec((1, d))],
        out_specs=tile,
        out_shape=jax.ShapeDtypeStruct((bsz, s, d), F32),
        scratch_shapes=[pltpu.VMEM((tm + 2 * POOL_HALO, d), F32)],
        compiler_params=_params(2),
        name="pool_mixer",
    )(x, x, x, w_in.astype(BF16), w_grp.astype(BF16), scale.reshape(1, d),
      w_out.astype(BF16), g.reshape(1, d), b.reshape(1, d))


def kernel(x, ln1_g, ln1_b, ffn_w1, ffn_b1, ffn_w2, ffn_b2, ln2_g, ln2_b,
           a_w_in, a_ln_g, a_ln_b, a_w_s, a_b_s, a_w_out,
           b_w_in, b_ln_g, b_ln_b, b_w_out,
           c_w_in, c_w_grp, c_scale, c_w_out):
    bsz, s, d = x.shape
    for i in range(DEPTH):
        kind, j = i % N_MIXERS, i // N_MIXERS
        if kind == 0:
            x = _gmlp_mixer(x.reshape(bsz * s, d), a_w_in[j], a_ln_g[j],
                            a_ln_b[j], a_w_s[j], a_b_s[j], a_w_out[j],
                            ln1_g[i], ln1_b[i])
        elif kind == 1:
            x = _fnet_mixer(x.reshape(bsz, s, d), b_w_in[j], b_ln_g[j],
                            b_ln_b[j], b_w_out[j], ln1_g[i], ln1_b[i])
        else:
            x = _pool_mixer(x.reshape(bsz, s, d), c_w_in[j], c_w_grp[j],
                            c_scale[j], c_w_out[j], ln1_g[i], ln1_b[i])
        x = _ffn(x.reshape(bsz * s, d), ffn_w1[i], ffn_b1[i], ffn_w2[i],
                 ffn_b2[i], ln2_g[i], ln2_b[i])
    return x.reshape(bsz, s, d)
```

```python
import functools
import math

import jax
import jax.numpy as jnp
import numpy as np
from jax import lax
from jax.experimental import pallas as pl
from jax.experimental.pallas import tpu as pltpu

DEPTH = 4
N_MIXERS = 3
DEEPNORM_ALPHA = (2.0 * DEPTH) ** 0.25
LN_EPS = 1e-5

A_CHUNK = 128
A_GROUPS = 8
B_GROUPS = 4
POOL_WINDOWS = (2, 4, 8, 16)
POOL_HALO = 8

V7X_VMEM_LIMIT_BYTES = 56 * 1024 * 1024
TOKEN_TILE = 512
GMLP_TILE = 512
GMLP_SPLIT = 1
FFN_TILE = 1024
FFN_CHUNK = 1024
DFT_RADIX = 4
DFT_TILE = 256
FNET_IN_TILE = 256

BF16 = jnp.bfloat16
F32 = jnp.float32


def _ln(r, g, b):
    mu = jnp.mean(r, axis=-1, keepdims=True)
    c = r - mu
    var = jnp.mean(c * c, axis=-1, keepdims=True)
    return c * lax.rsqrt(var + LN_EPS) * g + b


def _gelu_tanh(x):
    inner = math.sqrt(2.0 / math.pi) * (x + 0.044715 * (x * x * x))
    return 0.5 * x * (1.0 + jnp.tanh(inner))


def _dot(a, b):
    return jnp.dot(a, b, preferred_element_type=F32)


def _const_spec(shape):
    zeros = (0,) * len(shape)
    return pl.BlockSpec(shape, lambda *_: zeros, pipeline_mode=pl.Buffered(1))


def _params(n_axes):
    return pltpu.CompilerParams(
        dimension_semantics=("arbitrary",) * n_axes,
        vmem_limit_bytes=V7X_VMEM_LIMIT_BYTES)


def _ffn_kernel(x_ref, w1_ref, b1_ref, w2_ref, b2_ref, g_ref, b_ref, o_ref):
    x = x_ref[...]
    xb = x.astype(BF16)
    d_ff = w1_ref.shape[1]
    acc = jnp.zeros(x.shape, F32)
    for c in range(d_ff // FFN_CHUNK):
        cols = slice(c * FFN_CHUNK, (c + 1) * FFN_CHUNK)
        h = _dot(xb, w1_ref[:, cols]) + b1_ref[:, cols]
        h = jnp.square(jnp.maximum(h, 0.0))
        acc = acc + _dot(h.astype(BF16), w2_ref[cols, :])
    y = acc + b2_ref[...]
    o_ref[...] = _ln(DEEPNORM_ALPHA * x + y, g_ref[...], b_ref[...])


def _ffn(x2d, w1, b1, w2, b2, g, b):
    n, d = x2d.shape
    f = w1.shape[1]
    row = pl.BlockSpec((FFN_TILE, d), lambda i: (i, 0))
    return pl.pallas_call(
        _ffn_kernel,
        grid=(n // FFN_TILE,),
        in_specs=[row, _const_spec((d, f)), _const_spec((1, f)),
                  _const_spec((f, d)), _const_spec((1, d)),
                  _const_spec((1, d)), _const_spec((1, d))],
        out_specs=row,
        out_shape=jax.ShapeDtypeStruct((n, d), F32),
        compiler_params=_params(1),
        name="ffn",
    )(x2d, w1.astype(BF16), b1.reshape(1, f), w2.astype(BF16),
      b2.reshape(1, d), g.reshape(1, d), b.reshape(1, d))


def _gmlp_kernel(x_ref, w_in_ref, lng_ref, lnb_ref, ws_ref, bs_ref, w_out_ref,
                 g_ref, b_ref, o_ref):
    rows = x_ref.shape[0] // GMLP_SPLIT
    for blk in range(GMLP_SPLIT):
        sl = slice(blk * rows, (blk + 1) * rows)
        o_ref[sl, :] = _gmlp_rows(x_ref[sl, :], w_in_ref, lng_ref, lnb_ref,
                                  ws_ref, bs_ref, w_out_ref, g_ref, b_ref)


def _gmlp_rows(x, w_in_ref, lng_ref, lnb_ref, ws_ref, bs_ref, w_out_ref,
               g_ref, b_ref):
    width = w_out_ref.shape[0]
    gd = width // A_GROUPS
    n_chunks = x.shape[0] // A_CHUNK
    z = _gelu_tanh(_dot(x.astype(BF16), w_in_ref[...]))
    u = z[:, :width]
    v = _ln(z[:, width:], lng_ref[...], lnb_ref[...]).astype(BF16)
    mixed_groups = []
    for gi in range(A_GROUPS):
        cols = slice(gi * gd, (gi + 1) * gd)
        v_g = jnp.concatenate(
            [v[n * A_CHUNK:(n + 1) * A_CHUNK, cols] for n in range(n_chunks)],
            axis=1)
        mixed_groups.append(_dot(ws_ref[gi], v_g))
    rows = []
    for n in range(n_chunks):
        rows.append(jnp.concatenate(
            [m[:, n * gd:(n + 1) * gd] for m in mixed_groups], axis=1))
    mixed = jnp.concatenate(rows, axis=0)
    bias = jnp.concatenate([bs_ref[...]] * n_chunks, axis=0)
    out = u * (mixed + bias)
    y = _dot(out.astype(BF16), w_out_ref[...])
    return _ln(DEEPNORM_ALPHA * x + y, g_ref[...], b_ref[...])


def _gmlp_mixer(x2d, w_in, ln_g, ln_b, w_s, b_s, w_out, g, b):
    n, d = x2d.shape
    width = w_out.shape[0]
    gd = width // A_GROUPS
    bias_map = jnp.repeat(b_s.T, gd, axis=1)
    row = pl.BlockSpec((GMLP_TILE, d), lambda i: (i, 0))
    return pl.pallas_call(
        _gmlp_kernel,
        grid=(n // GMLP_TILE,),
        in_specs=[row, _const_spec((d, 2 * width)), _const_spec((1, width)),
                  _const_spec((1, width)),
                  _const_spec((A_GROUPS, A_CHUNK, A_CHUNK)),
                  _const_spec((A_CHUNK, width)), _const_spec((width, d)),
                  _const_spec((1, d)), _const_spec((1, d))],
        out_specs=row,
        out_shape=jax.ShapeDtypeStruct((n, d), F32),
        compiler_params=_params(1),
        name="gmlp_mixer",
    )(x2d, w_in.astype(BF16), ln_g.reshape(1, width), ln_b.reshape(1, width),
      w_s.astype(BF16), bias_map, w_out.astype(BF16),
      g.reshape(1, d), b.reshape(1, d))


def _fnet_in_kernel(x_ref, w_in_ref, lng_ref, lnb_ref, z_ref):
    d = w_in_ref.shape[0]
    rows = x_ref.shape[0]
    xs = jnp.concatenate(
        [x_ref[:, r * d:(r + 1) * d] for r in range(DFT_RADIX)], axis=0)
    z = _dot(xs.astype(BF16), w_in_ref[...])
    gd = d // B_GROUPS
    for gi in range(B_GROUPS):
        cols = slice(gi * gd, (gi + 1) * gd)
        zn = _ln(z[:, cols], lng_ref[:, cols], lnb_ref[:, cols]).astype(BF16)
        for r in range(DFT_RADIX):
            z_ref[0, r, :, cols] = zn[r * rows:(r + 1) * rows]


def _fnet_in(x, w_in, ln_g, ln_b):
    bsz, s, d = x.shape
    sq = s // DFT_RADIX
    x_wide = x.reshape(bsz, sq, DFT_RADIX * d)
    return pl.pallas_call(
        _fnet_in_kernel,
        grid=(bsz, sq // FNET_IN_TILE),
        in_specs=[pl.BlockSpec((None, FNET_IN_TILE, DFT_RADIX * d),
                               lambda bi, ti: (bi, ti, 0)),
                  _const_spec((d, d)), _const_spec((1, d)), _const_spec((1, d))],
        out_specs=pl.BlockSpec((1, DFT_RADIX, FNET_IN_TILE, d),
                               lambda bi, ti: (bi, 0, ti, 0)),
        out_shape=jax.ShapeDtypeStruct((bsz, DFT_RADIX, sq, d), BF16),
        compiler_params=_params(2),
        name="fnet_in",
    )(x_wide, w_in.astype(BF16), ln_g.reshape(1, d), ln_b.reshape(1, d))


def _fnet_dft_kernel(x_ref, z_ref, tc_ref, ts_ref, cdft_ref, w_out_ref,
                     g_ref, b_ref, o_ref, p_scr, q_scr):
    qi = pl.program_id(2)

    @pl.when(qi == 0)
    def _():
        tc = {r: _dot(tc_ref[r], z_ref[0, r]) for r in (0, 2, 1, 3)}
        ts = {r: _dot(ts_ref[r], z_ref[0, r]) for r in (0, 2, 1, 3)}
        c_ev_p, c_ev_m = tc[0] + tc[2], tc[0] - tc[2]
        c_od_p, c_od_m = tc[1] + tc[3], tc[1] - tc[3]
        s_ev_p, s_ev_m = ts[0] + ts[2], ts[0] - ts[2]
        s_od_p, s_od_m = ts[1] + ts[3], ts[1] - ts[3]
        p_scr[0] = (c_ev_p + c_od_p).astype(BF16)
        q_scr[0] = (s_ev_p + s_od_p).astype(BF16)
        p_scr[1] = (c_ev_m - s_od_m).astype(BF16)
        q_scr[1] = (s_ev_m + c_od_m).astype(BF16)
        p_scr[2] = (c_ev_p - c_od_p).astype(BF16)
        q_scr[2] = (s_ev_p - s_od_p).astype(BF16)
        p_scr[3] = (c_ev_m + s_od_m).astype(BF16)
        q_scr[3] = (s_ev_m - c_od_m).astype(BF16)

    x = x_ref[0, 0]
    p = p_scr[qi]
    q = q_scr[qi]
    gd = p.shape[1] // B_GROUPS
    f_groups = []
    for gi in range(B_GROUPS):
        cols = slice(gi * gd, (gi + 1) * gd)
        pq = jnp.concatenate([p[:, cols], q[:, cols]], axis=1)
        f_groups.append(_dot(pq, cdft_ref[...]))
    f = jnp.concatenate(f_groups, axis=1)
    y = _dot(f.astype(BF16), w_out_ref[...])
    o_ref[0, 0] = _ln(DEEPNORM_ALPHA * x + y, g_ref[...], b_ref[...])


def _cos_sin(index, period):
    ang = (index % period).astype(np.float64) * (2.0 * math.pi / period)
    return np.cos(ang), np.sin(ang)


def _position_dft_tables(s):
    sq = s // DFT_RADIX
    k = np.arange(sq, dtype=np.int64)[None, :, None]
    sp = np.arange(sq, dtype=np.int64)[None, None, :]
    r = np.arange(DFT_RADIX, dtype=np.int64)[:, None, None]
    return _cos_sin(k * (DFT_RADIX * sp + r), s)


def _channel_dft_table(gd, norm):
    j = np.arange(gd, dtype=np.int64)
    cos_c, sin_c = _cos_sin(j[:, None] * j[None, :], gd)
    return np.concatenate([cos_c, -sin_c], axis=0) * norm


def _fnet_dft(x, z, w_out, g, b):
    bsz, s, d = x.shape
    gd = d // B_GROUPS
    sq = s // DFT_RADIX
    tc, ts = _position_dft_tables(s)
    cdft = _channel_dft_table(gd, 1.0 / math.sqrt(s * gd))
    x4 = x.reshape(bsz, DFT_RADIX, sq, d)
    tile = pl.BlockSpec((1, 1, DFT_TILE, d), lambda bi, ki, qi: (bi, qi, ki, 0))
    seq = pl.BlockSpec((1, DFT_RADIX, sq, d), lambda bi, ki, qi: (bi, 0, 0, 0))
    table = pl.BlockSpec((DFT_RADIX, DFT_TILE, sq), lambda bi, ki, qi: (0, ki, 0))
    out = pl.pallas_call(
        _fnet_dft_kernel,
        grid=(bsz, sq // DFT_TILE, DFT_RADIX),
        in_specs=[tile, seq, table, table, _const_spec((2 * gd, gd)),
                  _const_spec((d, d)), _const_spec((1, d)), _const_spec((1, d))],
        out_specs=tile,
        out_shape=jax.ShapeDtypeStruct((bsz, DFT_RADIX, sq, d), F32),
        scratch_shapes=[pltpu.VMEM((DFT_RADIX, DFT_TILE, d), BF16),
                        pltpu.VMEM((DFT_RADIX, DFT_TILE, d), BF16)],
        compiler_params=_params(3),
        name="fnet_dft",
    )(x4, z, jnp.asarray(tc, BF16), jnp.asarray(ts, BF16),
      jnp.asarray(cdft, BF16), w_out.astype(BF16),
      g.reshape(1, d), b.reshape(1, d))
    return out.reshape(bsz, s, d)


def _fnet_mixer(x, w_in, ln_g, ln_b, w_out, g, b):
    d = x.shape[-1]
    z = _fnet_in(x, w_in, ln_g.reshape(d), ln_b.reshape(d))
    return _fnet_dft(x, z, w_out, g, b)


def _pool_kernel(x_ref, prev_ref, next_ref, w_in_ref, w_grp_ref, scale_ref,
                 w_out_ref, g_ref, b_ref, o_ref, z_scr, *, seq_len):
    ti = pl.program_id(1)
    tm = x_ref.shape[1]
    x = x_ref[0]
    first = ti == 0
    last = ti == pl.num_programs(1) - 1
    x_prev = jnp.where(first, 0.0, prev_ref[0])
    x_next = jnp.where(last, 0.0, next_ref[0])
    x_all = jnp.concatenate([x_prev, x, x_next], axis=0)
    z_scr[...] = _dot(x_all.astype(BF16), w_in_ref[...])

    t = ti * tm + lax.broadcasted_iota(jnp.int32, (tm, 1), 0)
    gd = x.shape[1] // len(POOL_WINDOWS)
    mixed_groups = []
    for gi, w in enumerate(POOL_WINDOWS):
        cols = slice(gi * gd, (gi + 1) * gd)
        half = w // 2
        win = z_scr[POOL_HALO - half:POOL_HALO - half + tm, cols]
        for off in range(1 - half, half):
            win = win + z_scr[POOL_HALO + off:POOL_HALO + off + tm, cols]
        lo = jnp.maximum(t - half, 0)
        hi = jnp.minimum(t - half + w, seq_len)
        count = (hi - lo).astype(F32)
        pooled = win / count - z_scr[POOL_HALO:POOL_HALO + tm, cols]
        mixed_groups.append(_dot(pooled.astype(BF16), w_grp_ref[gi]))
    mixed = jnp.concatenate(mixed_groups, axis=1) * scale_ref[...]
    y = _dot(mixed.astype(BF16), w_out_ref[...])
    o_ref[0] = _ln(DEEPNORM_ALPHA * x + y, g_ref[...], b_ref[...])


def _pool_mixer(x, w_in, w_grp, scale, w_out, g, b):
    bsz, s, d = x.shape
    n_g = len(POOL_WINDOWS)
    gd = d // n_g
    tm = TOKEN_TILE
    per_tile = tm // POOL_HALO
    n_halo = s // POOL_HALO
    tile = pl.BlockSpec((1, tm, d), lambda bi, ti: (bi, ti, 0))
    prev = pl.BlockSpec(
        (1, POOL_HALO, d),
        lambda bi, ti: (bi, jnp.maximum(ti * per_tile - 1, 0), 0))
    nxt = pl.BlockSpec(
        (1, POOL_HALO, d),
        lambda bi, ti: (bi, jnp.minimum((ti + 1) * per_tile, n_halo - 1), 0))
    return pl.pallas_call(
        functools.partial(_pool_kernel, seq_len=s),
        grid=(bsz, s // tm),
        in_specs=[tile, prev, nxt, _const_spec((d, d)),
                  _const_spec((n_g, gd, gd)), _const_spec((1, d)),
                  _const_spec((d, d)), _const_spec((1, d)), _const_spec((1, d))],
        out_specs=tile,
        out_shape=jax.ShapeDtypeStruct((bsz, s, d), F32),
        scratch_shapes=[pltpu.VMEM((tm + 2 * POOL_HALO, d), F32)],
        compiler_params=_params(2),
        name="pool_mixer",
    )(x, x, x, w_in.astype(BF16), w_grp.astype(BF16), scale.reshape(1, d),
      w_out.astype(BF16), g.reshape(1, d), b.reshape(1, d))


def kernel(x, ln1_g, ln1_b, ffn_w1, ffn_b1, ffn_w2, ffn_b2, ln2_g, ln2_b,
           a_w_in, a_ln_g, a_ln_b, a_w_s, a_b_s, a_w_out,
           b_w_in, b_ln_g, b_ln_b, b_w_out,
           c_w_in, c_w_grp, c_scale, c_w_out):
    bsz, s, d = x.shape
    for i in range(DEPTH):
        kind, j = i % N_MIXERS, i // N_MIXERS
        if kind == 0:
            x = _gmlp_mixer(x.reshape(bsz * s, d), a_w_in[j], a_ln_g[j],
                            a_ln_b[j], a_w_s[j], a_b_s[j], a_w_out[j],
                            ln1_g[i], ln1_b[i])
        elif kind == 1:
            x = _fnet_mixer(x.reshape(bsz, s, d), b_w_in[j], b_ln_g[j],
                            b_ln_b[j], b_w_out[j], ln1_g[i], ln1_b[i])
        else:
            x = _pool_mixer(x.reshape(bsz, s, d), c_w_in[j], c_w_grp[j],
                            c_scale[j], c_w_out[j], ln1_g[i], ln1_b[i])
        x = _ffn(x.reshape(bsz * s, d), ffn_w1[i], ffn_b1[i], ffn_w2[i],
                 ffn_b2[i], ln2_g[i], ln2_b[i])
    return x.reshape(bsz, s, d)
```

```python
import functools
import math

import jax
import jax.numpy as jnp
import numpy as np
from jax import lax
from jax.experimental import pallas as pl
from jax.experimental.pallas import tpu as pltpu

DEPTH = 4
N_MIXERS = 3
DEEPNORM_ALPHA = (2.0 * DEPTH) ** 0.25
LN_EPS = 1e-5

A_CHUNK = 128
A_GROUPS = 8
B_GROUPS = 4
POOL_WINDOWS = (2, 4, 8, 16)
LANES = 128
POOL_HALO = 8

V7X_VMEM_LIMIT_BYTES = 56 * 1024 * 1024
TOKEN_TILE = 512
GMLP_TILE = 512
GMLP_COLS = 256
FFN_TILE = 1024
FFN_CHUNK = 1024
DFT_RADIX = 4
DFT_TILE = 256
FNET_IN_TILE = 256

BF16 = jnp.bfloat16
F32 = jnp.float32


def _ln(r, g, b):
    mu = jnp.mean(r, axis=-1, keepdims=True)
    c = r - mu
    var = jnp.mean(c * c, axis=-1, keepdims=True)
    return c * lax.rsqrt(var + LN_EPS) * g + b


def _gelu_tanh(x):
    c = math.sqrt(2.0 / math.pi)
    half = 0.5 * x
    return half + half * jnp.tanh(x * (c + (c * 0.044715) * (x * x)))


def _dot(a, b):
    return jnp.dot(a, b, preferred_element_type=F32)


def _const_spec(shape):
    zeros = (0,) * len(shape)
    return pl.BlockSpec(shape, lambda *_: zeros, pipeline_mode=pl.Buffered(1))


def _params(n_axes):
    return pltpu.CompilerParams(
        dimension_semantics=("arbitrary",) * n_axes,
        vmem_limit_bytes=V7X_VMEM_LIMIT_BYTES)


def _ffn_kernel(x_ref, w1_ref, b1_ref, w2_ref, b2_ref, g_ref, b_ref, o_ref):
    x = x_ref[...]
    xb = x.astype(BF16)
    d_ff = w1_ref.shape[1]
    acc = jnp.zeros(x.shape, F32)
    for c in range(d_ff // FFN_CHUNK):
        cols = slice(c * FFN_CHUNK, (c + 1) * FFN_CHUNK)
        h = _dot(xb, w1_ref[:, cols]) + b1_ref[:, cols]
        h = jnp.square(jnp.maximum(h, 0.0))
        acc = acc + _dot(h.astype(BF16), w2_ref[cols, :])
    y = acc + b2_ref[...]
    o_ref[...] = _ln(DEEPNORM_ALPHA * x + y, g_ref[...], b_ref[...])


def _ffn(x2d, w1, b1, w2, b2, g, b):
    n, d = x2d.shape
    f = w1.shape[1]
    row = pl.BlockSpec((FFN_TILE, d), lambda i: (i, 0))
    return pl.pallas_call(
        _ffn_kernel,
        grid=(n // FFN_TILE,),
        in_specs=[row, _const_spec((d, f)), _const_spec((1, f)),
                  _const_spec((f, d)), _const_spec((1, d)),
                  _const_spec((1, d)), _const_spec((1, d))],
        out_specs=row,
        out_shape=jax.ShapeDtypeStruct((n, d), F32),
        compiler_params=_params(1),
        name="ffn",
    )(x2d, w1.astype(BF16), b1.reshape(1, f), w2.astype(BF16),
      b2.reshape(1, d), g.reshape(1, d), b.reshape(1, d))


def _gmlp_kernel(x_ref, w_in_ref, lng_ref, lnb_ref, ws_ref, bs_ref, w_out_ref,
                 g_ref, b_ref, o_ref):
    x = x_ref[...]
    xb = x.astype(BF16)
    width = w_out_ref.shape[0]
    gd = width // A_GROUPS
    n_chunks = x.shape[0] // A_CHUNK

    def proj(col0):
        return jnp.concatenate(
            [_gelu_tanh(_dot(xb, w_in_ref[:, c:c + GMLP_COLS]))
             for c in range(col0, col0 + width, GMLP_COLS)], axis=1)

    v = _ln(proj(width), lng_ref[...], lnb_ref[...]).astype(BF16)
    u = proj(0)
    mixed_groups = []
    for gi in range(A_GROUPS):
        cols = slice(gi * gd, (gi + 1) * gd)
        v_g = jnp.concatenate(
            [v[n * A_CHUNK:(n + 1) * A_CHUNK, cols] for n in range(n_chunks)],
            axis=1)
        mixed_groups.append(_dot(ws_ref[gi], v_g))
    for n in range(n_chunks):
        rows = slice(n * A_CHUNK, (n + 1) * A_CHUNK)
        mixed = jnp.concatenate(
            [m[:, n * gd:(n + 1) * gd] for m in mixed_groups], axis=1)
        out = u[rows] * (mixed + bs_ref[...])
        y = _dot(out.astype(BF16), w_out_ref[...])
        o_ref[rows, :] = _ln(DEEPNORM_ALPHA * x[rows] + y, g_ref[...], b_ref[...])


def _gmlp_mixer(x2d, w_in, ln_g, ln_b, w_s, b_s, w_out, g, b):
    n, d = x2d.shape
    width = w_out.shape[0]
    gd = width // A_GROUPS
    bias_map = jnp.repeat(b_s.T, gd, axis=1)
    row = pl.BlockSpec((GMLP_TILE, d), lambda i: (i, 0))
    return pl.pallas_call(
        _gmlp_kernel,
        grid=(n // GMLP_TILE,),
        in_specs=[row, _const_spec((d, 2 * width)), _const_spec((1, width)),
                  _const_spec((1, width)),
                  _const_spec((A_GROUPS, A_CHUNK, A_CHUNK)),
                  _const_spec((A_CHUNK, width)), _const_spec((width, d)),
                  _const_spec((1, d)), _const_spec((1, d))],
        out_specs=row,
        out_shape=jax.ShapeDtypeStruct((n, d), F32),
        compiler_params=_params(1),
        name="gmlp_mixer",
    )(x2d, w_in.astype(BF16), ln_g.reshape(1, width), ln_b.reshape(1, width),
      w_s.astype(BF16), bias_map, w_out.astype(BF16),
      g.reshape(1, d), b.reshape(1, d))


def _fnet_in_kernel(x_ref, w_in_ref, lng_ref, lnb_ref, z_ref, x_scr):
    d = w_in_ref.shape[0]
    rows = x_ref.shape[0] // DFT_RADIX
    n_lane_blocks = d // LANES
    for c in range(n_lane_blocks):
        x_scr[c] = x_ref[:, c * LANES:(c + 1) * LANES]
    xs = jnp.concatenate(
        [jnp.concatenate(
            [x_scr[pl.ds(c, 1), pl.ds(r, rows, stride=DFT_RADIX), :][0]
             for c in range(n_lane_blocks)], axis=1)
         for r in range(DFT_RADIX)], axis=0)
    z = _dot(xs.astype(BF16), w_in_ref[...])
    gd = d // B_GROUPS
    for gi in range(B_GROUPS):
        cols = slice(gi * gd, (gi + 1) * gd)
        zn = _ln(z[:, cols], lng_ref[:, cols], lnb_ref[:, cols]).astype(BF16)
        for r in range(DFT_RADIX):
            z_ref[0, r, :, cols] = zn[r * rows:(r + 1) * rows]


def _fnet_in(x, w_in, ln_g, ln_b):
    bsz, s, d = x.shape
    sq = s // DFT_RADIX
    return pl.pallas_call(
        _fnet_in_kernel,
        grid=(bsz, sq // FNET_IN_TILE),
        in_specs=[pl.BlockSpec((None, DFT_RADIX * FNET_IN_TILE, d),
                               lambda bi, ti: (bi, ti, 0)),
                  _const_spec((d, d)), _const_spec((1, d)), _const_spec((1, d))],
        out_specs=pl.BlockSpec((1, DFT_RADIX, FNET_IN_TILE, d),
                               lambda bi, ti: (bi, 0, ti, 0)),
        out_shape=jax.ShapeDtypeStruct((bsz, DFT_RADIX, sq, d), BF16),
        scratch_shapes=[pltpu.VMEM(
            (d // LANES, DFT_RADIX * FNET_IN_TILE, LANES), F32)],
        compiler_params=_params(2),
        name="fnet_in",
    )(x, w_in.astype(BF16), ln_g.reshape(1, d), ln_b.reshape(1, d))


def _fnet_dft_kernel(x_ref, z_ref, tc_ref, ts_ref, cdft_ref, w_out_ref,
                     g_ref, b_ref, o_ref, p_scr, q_scr):
    mi = pl.program_id(0)
    qi = pl.program_id(1)
    n_macro = pl.num_programs(0) - 1

    @pl.when(mi == 0)
    def _():
        o_ref[0, 0] = jnp.zeros(o_ref.shape[2:], F32)

    @pl.when(mi > 0)
    def _():
        slot = (mi - 1) % 2
        x = x_ref[0, 0]
        p = p_scr[slot, qi]
        q = q_scr[slot, qi]
        gd = p.shape[1] // B_GROUPS
        f_groups = []
        for gi in range(B_GROUPS):
            cols = slice(gi * gd, (gi + 1) * gd)
            pq = jnp.concatenate([p[:, cols], q[:, cols]], axis=1)
            f_groups.append(_dot(pq, cdft_ref[...]))
        f = jnp.concatenate(f_groups, axis=1)
        y = _dot(f.astype(BF16), w_out_ref[...])
        o_ref[0, 0] = _ln(DEEPNORM_ALPHA * x + y, g_ref[...], b_ref[...])

    @pl.when((qi == DFT_RADIX - 1) & (mi < n_macro))
    def _():
        slot = mi % 2
        tc = {r: _dot(tc_ref[r], z_ref[0, r]) for r in (0, 2, 1, 3)}
        ts = {r: _dot(ts_ref[r], z_ref[0, r]) for r in (0, 2, 1, 3)}
        c_ev_p, c_ev_m = tc[0] + tc[2], tc[0] - tc[2]
        c_od_p, c_od_m = tc[1] + tc[3], tc[1] - tc[3]
        s_ev_p, s_ev_m = ts[0] + ts[2], ts[0] - ts[2]
        s_od_p, s_od_m = ts[1] + ts[3], ts[1] - ts[3]
        p_scr[slot, 0] = (c_ev_p + c_od_p).astype(BF16)
        q_scr[slot, 0] = (s_ev_p + s_od_p).astype(BF16)
        p_scr[slot, 1] = (c_ev_m - s_od_m).astype(BF16)
        q_scr[slot, 1] = (s_ev_m + c_od_m).astype(BF16)
        p_scr[slot, 2] = (c_ev_p - c_od_p).astype(BF16)
        q_scr[slot, 2] = (s_ev_p - s_od_p).astype(BF16)
        p_scr[slot, 3] = (c_ev_m + s_od_m).astype(BF16)
        q_scr[slot, 3] = (s_ev_m - c_od_m).astype(BF16)


def _cos_sin(index, period):
    ang = (index % period).astype(np.float64) * (2.0 * math.pi / period)
    return np.cos(ang), np.sin(ang)


def _position_dft_tables(s):
    sq = s // DFT_RADIX
    k = np.arange(sq, dtype=np.int64)[None, :, None]
    sp = np.arange(sq, dtype=np.int64)[None, None, :]
    r = np.arange(DFT_RADIX, dtype=np.int64)[:, None, None]
    return _cos_sin(k * (DFT_RADIX * sp + r), s)


def _channel_dft_table(gd, norm):
    j = np.arange(gd, dtype=np.int64)
    cos_c, sin_c = _cos_sin(j[:, None] * j[None, :], gd)
    return np.concatenate([cos_c, -sin_c], axis=0) * norm


def _fnet_dft(x, z, w_out, g, b):
    bsz, s, d = x.shape
    gd = d // B_GROUPS
    sq = s // DFT_RADIX
    tc, ts = _position_dft_tables(s)
    cdft = _channel_dft_table(gd, 1.0 / math.sqrt(s * gd))
    x4 = x.reshape(bsz, DFT_RADIX, sq, d)
    n_kt = sq // DFT_TILE
    n_macro = bsz * n_kt

    def finishing(mi):
        return jnp.maximum(mi - 1, 0)

    def transforming(mi):
        return jnp.minimum(mi, n_macro - 1)

    tile = pl.BlockSpec(
        (1, 1, DFT_TILE, d),
        lambda mi, qi: (finishing(mi) // n_kt, qi, finishing(mi) % n_kt, 0))
    seq = pl.BlockSpec((1, DFT_RADIX, sq, d),
                       lambda mi, qi: (transforming(mi) // n_kt, 0, 0, 0))
    table = pl.BlockSpec((DFT_RADIX, DFT_TILE, sq),
                         lambda mi, qi: (0, transforming(mi) % n_kt, 0))
    out = pl.pallas_call(
        _fnet_dft_kernel,
        grid=(n_macro + 1, DFT_RADIX),
        in_specs=[tile, seq, table, table, _const_spec((2 * gd, gd)),
                  _const_spec((d, d)), _const_spec((1, d)), _const_spec((1, d))],
        out_specs=tile,
        out_shape=jax.ShapeDtypeStruct((bsz, DFT_RADIX, sq, d), F32),
        scratch_shapes=[pltpu.VMEM((2, DFT_RADIX, DFT_TILE, d), BF16),
                        pltpu.VMEM((2, DFT_RADIX, DFT_TILE, d), BF16)],
        compiler_params=_params(2),
        name="fnet_dft",
    )(x4, z, jnp.asarray(tc, BF16), jnp.asarray(ts, BF16),
      jnp.asarray(cdft, BF16), w_out.astype(BF16),
      g.reshape(1, d), b.reshape(1, d))
    return out.reshape(bsz, s, d)


def _fnet_mixer(x, w_in, ln_g, ln_b, w_out, g, b):
    d = x.shape[-1]
    z = _fnet_in(x, w_in, ln_g.reshape(d), ln_b.reshape(d))
    return _fnet_dft(x, z, w_out, g, b)


def _pool_kernel(x_ref, prev_ref, next_ref, w_in_ref, w_grp_ref, scale_ref,
                 w_out_ref, g_ref, b_ref, o_ref, z_scr, *, seq_len):
    ti = pl.program_id(1)
    tm = x_ref.shape[1]
    x = x_ref[0]
    first = ti == 0
    last = ti == pl.num_programs(1) - 1
    x_prev = jnp.where(first, 0.0, prev_ref[0])
    x_next = jnp.where(last, 0.0, next_ref[0])
    x_all = jnp.concatenate([x_prev, x, x_next], axis=0)
    z_scr[...] = _dot(x_all.astype(BF16), w_in_ref[...])

    t = ti * tm + lax.broadcasted_iota(jnp.int32, (tm, 1), 0)
    gd = x.shape[1] // len(POOL_WINDOWS)
    mixed_groups = []
    for gi, w in enumerate(POOL_WINDOWS):
        cols = slice(gi * gd, (gi + 1) * gd)
        half = w // 2
        win = z_scr[POOL_HALO - half:POOL_HALO - half + tm, cols]
        for off in range(1 - half, half):
            win = win + z_scr[POOL_HALO + off:POOL_HALO + off + tm, cols]
        lo = jnp.maximum(t - half, 0)
        hi = jnp.minimum(t - half + w, seq_len)
        count = (hi - lo).astype(F32)
        pooled = win / count - z_scr[POOL_HALO:POOL_HALO + tm, cols]
        mixed_groups.append(_dot(pooled.astype(BF16), w_grp_ref[gi]))
    mixed = jnp.concatenate(mixed_groups, axis=1) * scale_ref[...]
    y = _dot(mixed.astype(BF16), w_out_ref[...])
    o_ref[0] = _ln(DEEPNORM_ALPHA * x + y, g_ref[...], b_ref[...])


def _pool_mixer(x, w_in, w_grp, scale, w_out, g, b):
    bsz, s, d = x.shape
    n_g = len(POOL_WINDOWS)
    gd = d // n_g
    tm = TOKEN_TILE
    per_tile = tm // POOL_HALO
    n_halo = s // POOL_HALO
    tile = pl.BlockSpec((1, tm, d), lambda bi, ti: (bi, ti, 0))
    prev = pl.BlockSpec(
        (1, POOL_HALO, d),
        lambda bi, ti: (bi, jnp.maximum(ti * per_tile - 1, 0), 0))
    nxt = pl.BlockSpec(
        (1, POOL_HALO, d),
        lambda bi, ti: (bi, jnp.minimum((ti + 1) * per_tile, n_halo - 1), 0))
    return pl.pallas_call(
        functools.partial(_pool_kernel, seq_len=s),
        grid=(bsz, s // tm),
        in_specs=[tile, prev, nxt, _const_spec((d, d)),
                  _const_spec((n_g, gd, gd)), _const_spec((1, d)),
                  _const_spec((d, d)), _const_spec((1, d)), _const_spec((1, d))],
        out_specs=tile,
        out_shape=jax.ShapeDtypeStruct((bsz, s, d), F32),
        scratch_shapes=[pltpu.VMEM((tm + 2 * POOL_HALO, d), F32)],
        compiler_params=_params(2),
        name="pool_mixer",
    )(x, x, x, w_in.astype(BF16), w_grp.astype(BF16), scale.reshape(1, d),
      w_out.astype(BF16), g.reshape(1, d), b.reshape(1, d))


def kernel(x, ln1_g, ln1_b, ffn_w1, ffn_b1, ffn_w2, ffn_b2, ln2_g, ln2_b,
           a_w_in, a_ln_g, a_ln_b, a_w_s, a_b_s, a_w_out,
           b_w_in, b_ln_g, b_ln_b, b_w_out,
           c_w_in, c_w_grp, c_scale, c_w_out):
    bsz, s, d = x.shape
    for i in range(DEPTH):
        kind, j = i % N_MIXERS, i // N_MIXERS
        if kind == 0:
            x = _gmlp_mixer(x.reshape(bsz * s, d), a_w_in[j], a_ln_g[j],
                            a_ln_b[j], a_w_s[j], a_b_s[j], a_w_out[j],
                            ln1_g[i], ln1_b[i])
        elif kind == 1:
            x = _fnet_mixer(x.reshape(bsz, s, d), b_w_in[j], b_ln_g[j],
                            b_ln_b[j], b_w_out[j], ln1_g[i], ln1_b[i])
        else:
            x = _pool_mixer(x.reshape(bsz, s, d), c_w_in[j], c_w_grp[j],
                            c_scale[j], c_w_out[j], ln1_g[i], ln1_b[i])
        x = _ffn(x.reshape(bsz * s, d), ffn_w1[i], ffn_b1[i], ffn_w2[i],
                 ffn_b2[i], ln2_g[i], ln2_b[i])
    return x.reshape(bsz, s, d)
```

```python
import functools
import math

import jax
import jax.numpy as jnp
import numpy as np
from jax import lax
from jax.experimental import pallas as pl
from jax.experimental.pallas import tpu as pltpu

DEPTH = 4
N_MIXERS = 3
DEEPNORM_ALPHA = (2.0 * DEPTH) ** 0.25
LN_EPS = 1e-5

A_CHUNK = 128
A_GROUPS = 8
B_GROUPS = 4
POOL_WINDOWS = (2, 4, 8, 16)
POOL_ROWS = 128
LANES = 128
POOL_HALO = 16
POOL_QUAD = 4

V7X_VMEM_LIMIT_BYTES = 56 * 1024 * 1024
POOL_TILE = 1024
GMLP_TILE = 1024
GMLP_COLS = 256
FFN_TILE = 2048
FFN_ROWS = 256
FFN_CHUNK = 1024
DFT_RADIX = 4
DFT_TILE = 256
FNET_IN_TILE = 256

BF16 = jnp.bfloat16
F32 = jnp.float32


def _ln(r, g, b):
    mu = jnp.mean(r, axis=-1, keepdims=True)
    c = r - mu
    var = jnp.mean(c * c, axis=-1, keepdims=True)
    return c * lax.rsqrt(var + LN_EPS) * g + b


def _gelu_tanh(x):
    c = math.sqrt(2.0 / math.pi)
    half = 0.5 * x
    return half + half * jnp.tanh(x * (c + (c * 0.044715) * (x * x)))


def _dot(a, b):
    return jnp.dot(a, b, preferred_element_type=F32)


def _const_spec(shape):
    zeros = (0,) * len(shape)
    return pl.BlockSpec(shape, lambda *_: zeros, pipeline_mode=pl.Buffered(1))


def _params(n_axes):
    return pltpu.CompilerParams(
        dimension_semantics=("arbitrary",) * n_axes,
        vmem_limit_bytes=V7X_VMEM_LIMIT_BYTES)


def _ffn_kernel(x_ref, w1_ref, b1_ref, w2_ref, b2_ref, g_ref, b_ref, o_ref):
    d_ff = w1_ref.shape[1]
    for rb in range(x_ref.shape[0] // FFN_ROWS):
        rows = slice(rb * FFN_ROWS, (rb + 1) * FFN_ROWS)
        x = x_ref[rows, :]
        xb = x.astype(BF16)
        acc = jnp.zeros(x.shape, F32)
        for c in range(d_ff // FFN_CHUNK):
            cols = slice(c * FFN_CHUNK, (c + 1) * FFN_CHUNK)
            h = _dot(xb, w1_ref[:, cols]) + b1_ref[:, cols]
            h = jnp.square(jnp.maximum(h, 0.0))
            acc = acc + _dot(h.astype(BF16), w2_ref[cols, :])
        y = acc + b2_ref[...]
        o_ref[rows, :] = _ln(DEEPNORM_ALPHA * x + y, g_ref[...], b_ref[...])


def _ffn(x2d, w1, b1, w2, b2, g, b):
    n, d = x2d.shape
    f = w1.shape[1]
    row = pl.BlockSpec((FFN_TILE, d), lambda i: (i, 0))
    return pl.pallas_call(
        _ffn_kernel,
        grid=(n // FFN_TILE,),
        in_specs=[row, _const_spec((d, f)), _const_spec((1, f)),
                  _const_spec((f, d)), _const_spec((1, d)),
                  _const_spec((1, d)), _const_spec((1, d))],
        out_specs=row,
        out_shape=jax.ShapeDtypeStruct((n, d), F32),
        compiler_params=_params(1),
        name="ffn",
    )(x2d, w1.astype(BF16), b1.reshape(1, f), w2.astype(BF16),
      b2.reshape(1, d), g.reshape(1, d), b.reshape(1, d))


def _gmlp_kernel(x_ref, w_in_ref, lng_ref, lnb_ref, ws_ref, bs_ref, w_out_ref,
                 g_ref, b_ref, o_ref):
    x = x_ref[...]
    xb = x.astype(BF16)
    width = w_out_ref.shape[0]
    gd = width // A_GROUPS
    n_chunks = x.shape[0] // A_CHUNK

    def proj(col0):
        return jnp.concatenate(
            [_gelu_tanh(_dot(xb, w_in_ref[:, c:c + GMLP_COLS]))
             for c in range(col0, col0 + width, GMLP_COLS)], axis=1)

    v = _ln(proj(width), lng_ref[...], lnb_ref[...]).astype(BF16)
    u = proj(0)
    mixed_groups = []
    for gi in range(A_GROUPS):
        cols = slice(gi * gd, (gi + 1) * gd)
        v_g = jnp.concatenate(
            [v[n * A_CHUNK:(n + 1) * A_CHUNK, cols] for n in range(n_chunks)],
            axis=1)
        mixed_groups.append(_dot(ws_ref[gi], v_g))
    for n in range(n_chunks):
        rows = slice(n * A_CHUNK, (n + 1) * A_CHUNK)
        mixed = jnp.concatenate(
            [m[:, n * gd:(n + 1) * gd] for m in mixed_groups], axis=1)
        out = u[rows] * (mixed + bs_ref[...])
        y = _dot(out.astype(BF16), w_out_ref[...])
        o_ref[rows, :] = _ln(DEEPNORM_ALPHA * x[rows] + y, g_ref[...], b_ref[...])


def _gmlp_mixer(x2d, w_in, ln_g, ln_b, w_s, b_s, w_out, g, b):
    n, d = x2d.shape
    width = w_out.shape[0]
    gd = width // A_GROUPS
    bias_map = jnp.repeat(b_s.T, gd, axis=1)
    row = pl.BlockSpec((GMLP_TILE, d), lambda i: (i, 0))
    return pl.pallas_call(
        _gmlp_kernel,
        grid=(n // GMLP_TILE,),
        in_specs=[row, _const_spec((d, 2 * width)), _const_spec((1, width)),
                  _const_spec((1, width)),
                  _const_spec((A_GROUPS, A_CHUNK, A_CHUNK)),
                  _const_spec((A_CHUNK, width)), _const_spec((width, d)),
                  _const_spec((1, d)), _const_spec((1, d))],
        out_specs=row,
        out_shape=jax.ShapeDtypeStruct((n, d), F32),
        compiler_params=_params(1),
        name="gmlp_mixer",
    )(x2d, w_in.astype(BF16), ln_g.reshape(1, width), ln_b.reshape(1, width),
      w_s.astype(BF16), bias_map, w_out.astype(BF16),
      g.reshape(1, d), b.reshape(1, d))


def _fnet_in_kernel(x_ref, w_in_ref, lng_ref, lnb_ref, z_ref, x_scr):
    d = w_in_ref.shape[0]
    rows = x_ref.shape[0] // DFT_RADIX
    n_lane_blocks = d // LANES
    for c in range(n_lane_blocks):
        x_scr[c] = x_ref[:, c * LANES:(c + 1) * LANES]
    xs = jnp.concatenate(
        [jnp.concatenate(
            [x_scr[pl.ds(c, 1), pl.ds(r, rows, stride=DFT_RADIX), :][0]
             for c in range(n_lane_blocks)], axis=1)
         for r in range(DFT_RADIX)], axis=0)
    z = _dot(xs.astype(BF16), w_in_ref[...])
    gd = d // B_GROUPS
    for gi in range(B_GROUPS):
        cols = slice(gi * gd, (gi + 1) * gd)
        zn = _ln(z[:, cols], lng_ref[:, cols], lnb_ref[:, cols]).astype(BF16)
        for r in range(DFT_RADIX):
            z_ref[0, r, :, cols] = zn[r * rows:(r + 1) * rows]


def _fnet_in(x, w_in, ln_g, ln_b):
    bsz, s, d = x.shape
    sq = s // DFT_RADIX
    return pl.pallas_call(
        _fnet_in_kernel,
        grid=(bsz, sq // FNET_IN_TILE),
        in_specs=[pl.BlockSpec((None, DFT_RADIX * FNET_IN_TILE, d),
                               lambda bi, ti: (bi, ti, 0)),
                  _const_spec((d, d)), _const_spec((1, d)), _const_spec((1, d))],
        out_specs=pl.BlockSpec((1, DFT_RADIX, FNET_IN_TILE, d),
                               lambda bi, ti: (bi, 0, ti, 0)),
        out_shape=jax.ShapeDtypeStruct((bsz, DFT_RADIX, sq, d), BF16),
        scratch_shapes=[pltpu.VMEM(
            (d // LANES, DFT_RADIX * FNET_IN_TILE, LANES), F32)],
        compiler_params=_params(2),
        name="fnet_in",
    )(x, w_in.astype(BF16), ln_g.reshape(1, d), ln_b.reshape(1, d))


def _fnet_dft_kernel(x_ref, z_ref, tc_ref, ts_ref, cdft_ref, w_out_ref,
                     g_ref, b_ref, o_ref, p_scr, q_scr):
    mi = pl.program_id(0)
    qi = pl.program_id(1)
    n_macro = pl.num_programs(0) - 1

    @pl.when(mi == 0)
    def _():
        o_ref[0, 0] = jnp.zeros(o_ref.shape[2:], F32)

    @pl.when(mi > 0)
    def _():
        slot = (mi - 1) % 2
        x = x_ref[0, 0]
        p = p_scr[slot, qi]
        q = q_scr[slot, qi]
        gd = p.shape[1] // B_GROUPS
        f_groups = []
        for gi in range(B_GROUPS):
            cols = slice(gi * gd, (gi + 1) * gd)
            pq = jnp.concatenate([p[:, cols], q[:, cols]], axis=1)
            f_groups.append(_dot(pq, cdft_ref[...]))
        f = jnp.concatenate(f_groups, axis=1)
        y = _dot(f.astype(BF16), w_out_ref[...])
        o_ref[0, 0] = _ln(DEEPNORM_ALPHA * x + y, g_ref[...], b_ref[...])

    @pl.when((qi == DFT_RADIX - 1) & (mi < n_macro))
    def _():
        slot = mi % 2
        tc = {r: _dot(tc_ref[r], z_ref[0, r]) for r in (0, 2, 1, 3)}
        ts = {r: _dot(ts_ref[r], z_ref[0, r]) for r in (0, 2, 1, 3)}
        c_ev_p, c_ev_m = tc[0] + tc[2], tc[0] - tc[2]
        c_od_p, c_od_m = tc[1] + tc[3], tc[1] - tc[3]
        s_ev_p, s_ev_m = ts[0] + ts[2], ts[0] - ts[2]
        s_od_p, s_od_m = ts[1] + ts[3], ts[1] - ts[3]
        p_scr[slot, 0] = (c_ev_p + c_od_p).astype(BF16)
        q_scr[slot, 0] = (s_ev_p + s_od_p).astype(BF16)
        p_scr[slot, 1] = (c_ev_m - s_od_m).astype(BF16)
        q_scr[slot, 1] = (s_ev_m + c_od_m).astype(BF16)
        p_scr[slot, 2] = (c_ev_p - c_od_p).astype(BF16)
        q_scr[slot, 2] = (s_ev_p - s_od_p).astype(BF16)
        p_scr[slot, 3] = (c_ev_m + s_od_m).astype(BF16)
        q_scr[slot, 3] = (s_ev_m - c_od_m).astype(BF16)


def _cos_sin(index, period):
    ang = (index % period).astype(np.float64) * (2.0 * math.pi / period)
    return np.cos(ang), np.sin(ang)


def _position_dft_tables(s):
    sq = s // DFT_RADIX
    k = np.arange(sq, dtype=np.int64)[None, :, None]
    sp = np.arange(sq, dtype=np.int64)[None, None, :]
    r = np.arange(DFT_RADIX, dtype=np.int64)[:, None, None]
    return _cos_sin(k * (DFT_RADIX * sp + r), s)


def _channel_dft_table(gd, norm):
    j = np.arange(gd, dtype=np.int64)
    cos_c, sin_c = _cos_sin(j[:, None] * j[None, :], gd)
    return np.concatenate([cos_c, -sin_c], axis=0) * norm


def _fnet_dft(x, z, w_out, g, b):
    bsz, s, d = x.shape
    gd = d // B_GROUPS
    sq = s // DFT_RADIX
    tc, ts = _position_dft_tables(s)
    cdft = _channel_dft_table(gd, 1.0 / math.sqrt(s * gd))
    x4 = x.reshape(bsz, DFT_RADIX, sq, d)
    n_kt = sq // DFT_TILE
    n_macro = bsz * n_kt

    def finishing(mi):
        return jnp.maximum(mi - 1, 0)

    def transforming(mi):
        return jnp.minimum(mi, n_macro - 1)

    tile = pl.BlockSpec(
        (1, 1, DFT_TILE, d),
        lambda mi, qi: (finishing(mi) // n_kt, qi, finishing(mi) % n_kt, 0))
    seq = pl.BlockSpec((1, DFT_RADIX, sq, d),
                       lambda mi, qi: (transforming(mi) // n_kt, 0, 0, 0))
    table = pl.BlockSpec((DFT_RADIX, DFT_TILE, sq),
                         lambda mi, qi: (0, transforming(mi) % n_kt, 0))
    out = pl.pallas_call(
        _fnet_dft_kernel,
        grid=(n_macro + 1, DFT_RADIX),
        in_specs=[tile, seq, table, table, _const_spec((2 * gd, gd)),
                  _const_spec((d, d)), _const_spec((1, d)), _const_spec((1, d))],
        out_specs=tile,
        out_shape=jax.ShapeDtypeStruct((bsz, DFT_RADIX, sq, d), F32),
        scratch_shapes=[pltpu.VMEM((2, DFT_RADIX, DFT_TILE, d), BF16),
                        pltpu.VMEM((2, DFT_RADIX, DFT_TILE, d), BF16)],
        compiler_params=_params(2),
        name="fnet_dft",
    )(x4, z, jnp.asarray(tc, BF16), jnp.asarray(ts, BF16),
      jnp.asarray(cdft, BF16), w_out.astype(BF16),
      g.reshape(1, d), b.reshape(1, d))
    return out.reshape(bsz, s, d)


def _fnet_mixer(x, w_in, ln_g, ln_b, w_out, g, b):
    d = x.shape[-1]
    z = _fnet_in(x, w_in, ln_g.reshape(d), ln_b.reshape(d))
    return _fnet_dft(x, z, w_out, g, b)


def _pool_kernel(x_ref, prev_ref, next_ref, w_in_ref, w_grp_ref, scale_ref,
                 w_out_ref, g_ref, b_ref, o_ref, z_scr, quad_scr, *, seq_len):
    ti = pl.program_id(1)
    tm = x_ref.shape[1]
    x = x_ref[0]
    first = ti == 0
    last = ti == pl.num_programs(1) - 1
    x_prev = jnp.where(first, 0.0, prev_ref[0])
    x_next = jnp.where(last, 0.0, next_ref[0])
    x_all = jnp.concatenate([x_prev, x, x_next], axis=0)
    z_scr[...] = _dot(x_all.astype(BF16), w_in_ref[...])

    t = ti * tm + lax.broadcasted_iota(jnp.int32, (tm, 1), 0)
    gd = x.shape[1] // len(POOL_WINDOWS)
    wide = [gi for gi, w in enumerate(POOL_WINDOWS) if w >= 2 * POOL_QUAD]
    wide_col0 = wide[0] * gd
    max_half = max(POOL_WINDOWS) // 2
    quad_row0 = POOL_HALO - max_half
    quad_rows = tm + 2 * max_half
    quad = z_scr[quad_row0:quad_row0 + quad_rows, wide_col0:]
    for k in range(1, POOL_QUAD):
        quad = quad + z_scr[quad_row0 + k:quad_row0 + k + quad_rows, wide_col0:]
    quad_scr[quad_row0:quad_row0 + quad_rows, :] = quad

    mixed_groups = []
    for gi, w in enumerate(POOL_WINDOWS):
        cols = slice(gi * gd, (gi + 1) * gd)
        half = w // 2
        row0 = POOL_HALO - half
        if gi in wide:
            qcols = slice(gi * gd - wide_col0, (gi + 1) * gd - wide_col0)
            win = quad_scr[row0:row0 + tm, qcols]
            for m in range(1, w // POOL_QUAD):
                r = row0 + m * POOL_QUAD
                win = win + quad_scr[r:r + tm, qcols]
        else:
            win = z_scr[row0:row0 + tm, cols]
            for k in range(1, w):
                win = win + z_scr[row0 + k:row0 + k + tm, cols]
        lo = jnp.maximum(t - half, 0)
        hi = jnp.minimum(t - half + w, seq_len)
        inv_count = 1.0 / (hi - lo).astype(F32)
        pooled = win * inv_count - z_scr[POOL_HALO:POOL_HALO + tm, cols]
        mixed_g = _dot(pooled.astype(BF16), w_grp_ref[gi]) * scale_ref[:, cols]
        mixed_groups.append(mixed_g.astype(BF16))
    mixed = jnp.concatenate(mixed_groups, axis=1)
    for rb in range(tm // POOL_ROWS):
        rows = slice(rb * POOL_ROWS, (rb + 1) * POOL_ROWS)
        y = _dot(mixed[rows], w_out_ref[...])
        o_ref[0, rows, :] = _ln(DEEPNORM_ALPHA * x[rows] + y, g_ref[...],
                                b_ref[...])


def _pool_mixer(x, w_in, w_grp, scale, w_out, g, b):
    bsz, s, d = x.shape
    n_g = len(POOL_WINDOWS)
    gd = d // n_g
    tm = POOL_TILE
    n_narrow = sum(w < 2 * POOL_QUAD for w in POOL_WINDOWS)
    per_tile = tm // POOL_HALO
    n_halo = s // POOL_HALO
    tile = pl.BlockSpec((1, tm, d), lambda bi, ti: (bi, ti, 0))
    prev = pl.BlockSpec(
        (1, POOL_HALO, d),
        lambda bi, ti: (bi, jnp.maximum(ti * per_tile - 1, 0), 0))
    nxt = pl.BlockSpec(
        (1, POOL_HALO, d),
        lambda bi, ti: (bi, jnp.minimum((ti + 1) * per_tile, n_halo - 1), 0))
    return pl.pallas_call(
        functools.partial(_pool_kernel, seq_len=s),
        grid=(bsz, s // tm),
        in_specs=[tile, prev, nxt, _const_spec((d, d)),
                  _const_spec((n_g, gd, gd)), _const_spec((1, d)),
                  _const_spec((d, d)), _const_spec((1, d)), _const_spec((1, d))],
        out_specs=tile,
        out_shape=jax.ShapeDtypeStruct((bsz, s, d), F32),
        scratch_shapes=[
            pltpu.VMEM((tm + 2 * POOL_HALO, d), F32),
            pltpu.VMEM((tm + 2 * POOL_HALO, d - n_narrow * gd), F32)],
        compiler_params=_params(2),
        name="pool_mixer",
    )(x, x, x, w_in.astype(BF16), w_grp.astype(BF16), scale.reshape(1, d),
      w_out.astype(BF16), g.reshape(1, d), b.reshape(1, d))


def kernel(x, ln1_g, ln1_b, ffn_w1, ffn_b1, ffn_w2, ffn_b2, ln2_g, ln2_b,
           a_w_in, a_ln_g, a_ln_b, a_w_s, a_b_s, a_w_out,
           b_w_in, b_ln_g, b_ln_b, b_w_out,
           c_w_in, c_w_grp, c_scale, c_w_out):
    bsz, s, d = x.shape
    for i in range(DEPTH):
        kind, j = i % N_MIXERS, i // N_MIXERS
        if kind == 0:
            x = _gmlp_mixer(x.reshape(bsz * s, d), a_w_in[j], a_ln_g[j],
                            a_ln_b[j], a_w_s[j], a_b_s[j], a_w_out[j],
                            ln1_g[i], ln1_b[i])
        elif kind == 1:
            x = _fnet_mixer(x.reshape(bsz, s, d), b_w_in[j], b_ln_g[j],
                            b_ln_b[j], b_w_out[j], ln1_g[i], ln1_b[i])
        else:
            x = _pool_mixer(x.reshape(bsz, s, d), c_w_in[j], c_w_grp[j],
                            c_scale[j], c_w_out[j], ln1_g[i], ln1_b[i])
        x = _ffn(x.reshape(bsz * s, d), ffn_w1[i], ffn_b1[i], ffn_w2[i],
                 ffn_b2[i], ln2_g[i], ln2_b[i])
    return x.reshape(bsz, s, d)
```

```python
import functools
import math

import jax
import jax.numpy as jnp
import numpy as np
from jax import lax
from jax.experimental import pallas as pl
from jax.experimental.pallas import tpu as pltpu

DEPTH = 4
N_MIXERS = 3
DEEPNORM_ALPHA = (2.0 * DEPTH) ** 0.25
LN_EPS = 1e-5

A_CHUNK = 128
A_GROUPS = 8
B_GROUPS = 4
POOL_WINDOWS = (2, 4, 8, 16)
POOL_ROWS = 128
LANES = 128
POOL_HALO = 16
POOL_QUAD = 4

V7X_VMEM_LIMIT_BYTES = 56 * 1024 * 1024
POOL_TILE = 1024
GMLP_TILE = 1024
GMLP_COLS = 256
FFN_TILE = 1024
FFN_ROWS = 256
FFN_CHUNK = 1024
DFT_RADIX = 4
DFT_TILE = 256
DFT_QSTEP = 2
FNET_IN_TILE = 256

BF16 = jnp.bfloat16
F32 = jnp.float32


def _ln(r, g, b):
    mu = jnp.mean(r, axis=-1, keepdims=True)
    c = r - mu
    var = jnp.mean(c * c, axis=-1, keepdims=True)
    return c * lax.rsqrt(var + LN_EPS) * g + b


def _gelu_tanh(x):
    c = math.sqrt(2.0 / math.pi)
    half = 0.5 * x
    return half + half * jnp.tanh(x * (c + (c * 0.044715) * (x * x)))


def _dot(a, b):
    return jnp.dot(a, b, preferred_element_type=F32)


def _const_spec(shape):
    zeros = (0,) * len(shape)
    return pl.BlockSpec(shape, lambda *_: zeros, pipeline_mode=pl.Buffered(1))


def _params(n_axes):
    return pltpu.CompilerParams(
        dimension_semantics=("arbitrary",) * n_axes,
        vmem_limit_bytes=V7X_VMEM_LIMIT_BYTES)


def _ffn_kernel(x_ref, w1_ref, b1_ref, w2_ref, b2_ref, g_ref, b_ref, o_ref):
    d_ff = w1_ref.shape[1]
    n_chunks = d_ff // FFN_CHUNK
    x = x_ref[...]
    xb = x.astype(BF16)
    acc = jnp.zeros(x.shape, F32)
    for c in range(n_chunks):
        cols = slice(c * FFN_CHUNK, (c + 1) * FFN_CHUNK)
        h = _dot(xb, w1_ref[:, cols]) + b1_ref[:, cols]
        h = jnp.square(jnp.maximum(h, 0.0)).astype(BF16)
        if c < n_chunks - 1:
            acc = acc + _dot(h, w2_ref[cols, :])
            continue
        for rb in range(x.shape[0] // FFN_ROWS):
            rows = slice(rb * FFN_ROWS, (rb + 1) * FFN_ROWS)
            y = acc[rows] + _dot(h[rows], w2_ref[cols, :]) + b2_ref[...]
            o_ref[rows, :] = _ln(DEEPNORM_ALPHA * x[rows] + y, g_ref[...],
                                 b_ref[...])


def _ffn(x2d, w1, b1, w2, b2, g, b):
    n, d = x2d.shape
    f = w1.shape[1]
    row = pl.BlockSpec((FFN_TILE, d), lambda i: (i, 0))
    return pl.pallas_call(
        _ffn_kernel,
        grid=(n // FFN_TILE,),
        in_specs=[row, _const_spec((d, f)), _const_spec((1, f)),
                  _const_spec((f, d)), _const_spec((1, d)),
                  _const_spec((1, d)), _const_spec((1, d))],
        out_specs=row,
        out_shape=jax.ShapeDtypeStruct((n, d), F32),
        compiler_params=_params(1),
        name="ffn",
    )(x2d, w1.astype(BF16), b1.reshape(1, f), w2.astype(BF16),
      b2.reshape(1, d), g.reshape(1, d), b.reshape(1, d))


def _gmlp_kernel(x_ref, w_in_ref, lng_ref, lnb_ref, ws_ref, bs_ref, w_out_ref,
                 g_ref, b_ref, o_ref):
    x = x_ref[...]
    xb = x.astype(BF16)
    width = w_out_ref.shape[0]
    gd = width // A_GROUPS
    n_chunks = x.shape[0] // A_CHUNK

    def proj(col0):
        return jnp.concatenate(
            [_gelu_tanh(_dot(xb, w_in_ref[:, c:c + GMLP_COLS]))
             for c in range(col0, col0 + width, GMLP_COLS)], axis=1)

    v = _ln(proj(width), lng_ref[...], lnb_ref[...]).astype(BF16)
    u = proj(0)
    mixed_groups = []
    for gi in range(A_GROUPS):
        cols = slice(gi * gd, (gi + 1) * gd)
        v_g = jnp.concatenate(
            [v[n * A_CHUNK:(n + 1) * A_CHUNK, cols] for n in range(n_chunks)],
            axis=1)
        mixed_groups.append(_dot(ws_ref[gi], v_g))
    for n in range(n_chunks):
        rows = slice(n * A_CHUNK, (n + 1) * A_CHUNK)
        mixed = jnp.concatenate(
            [m[:, n * gd:(n + 1) * gd] for m in mixed_groups], axis=1)
        out = u[rows] * (mixed + bs_ref[...])
        y = _dot(out.astype(BF16), w_out_ref[...])
        o_ref[rows, :] = _ln(DEEPNORM_ALPHA * x[rows] + y, g_ref[...], b_ref[...])


def _gmlp_mixer(x2d, w_in, ln_g, ln_b, w_s, b_s, w_out, g, b):
    n, d = x2d.shape
    width = w_out.shape[0]
    gd = width // A_GROUPS
    bias_map = jnp.repeat(b_s.T, gd, axis=1)
    row = pl.BlockSpec((GMLP_TILE, d), lambda i: (i, 0))
    return pl.pallas_call(
        _gmlp_kernel,
        grid=(n // GMLP_TILE,),
        in_specs=[row, _const_spec((d, 2 * width)), _const_spec((1, width)),
                  _const_spec((1, width)),
                  _const_spec((A_GROUPS, A_CHUNK, A_CHUNK)),
                  _const_spec((A_CHUNK, width)), _const_spec((width, d)),
                  _const_spec((1, d)), _const_spec((1, d))],
        out_specs=row,
        out_shape=jax.ShapeDtypeStruct((n, d), F32),
        compiler_params=_params(1),
        name="gmlp_mixer",
    )(x2d, w_in.astype(BF16), ln_g.reshape(1, width), ln_b.reshape(1, width),
      w_s.astype(BF16), bias_map, w_out.astype(BF16),
      g.reshape(1, d), b.reshape(1, d))


def _fnet_in_kernel(x_ref, w_in_ref, lng_ref, lnb_ref, z_ref, x_scr):
    d = w_in_ref.shape[0]
    rows = x_ref.shape[0] // DFT_RADIX
    n_lane_blocks = d // LANES
    for c in range(n_lane_blocks):
        x_scr[c] = x_ref[:, c * LANES:(c + 1) * LANES]
    xs = jnp.concatenate(
        [jnp.concatenate(
            [x_scr[pl.ds(c, 1), pl.ds(r, rows, stride=DFT_RADIX), :][0]
             for c in range(n_lane_blocks)], axis=1)
         for r in range(DFT_RADIX)], axis=0)
    z = _dot(xs.astype(BF16), w_in_ref[...])
    gd = d // B_GROUPS
    for gi in range(B_GROUPS):
        cols = slice(gi * gd, (gi + 1) * gd)
        zn = _ln(z[:, cols], lng_ref[:, cols], lnb_ref[:, cols]).astype(BF16)
        for r in range(DFT_RADIX):
            z_ref[0, r, :, cols] = zn[r * rows:(r + 1) * rows]


def _fnet_in(x, w_in, ln_g, ln_b):
    bsz, s, d = x.shape
    sq = s // DFT_RADIX
    return pl.pallas_call(
        _fnet_in_kernel,
        grid=(bsz, sq // FNET_IN_TILE),
        in_specs=[pl.BlockSpec((None, DFT_RADIX * FNET_IN_TILE, d),
                               lambda bi, ti: (bi, ti, 0)),
                  _const_spec((d, d)), _const_spec((1, d)), _const_spec((1, d))],
        out_specs=pl.BlockSpec((1, DFT_RADIX, FNET_IN_TILE, d),
                               lambda bi, ti: (bi, 0, ti, 0)),
        out_shape=jax.ShapeDtypeStruct((bsz, DFT_RADIX, sq, d), BF16),
        scratch_shapes=[pltpu.VMEM(
            (d // LANES, DFT_RADIX * FNET_IN_TILE, LANES), F32)],
        compiler_params=_params(2),
        name="fnet_in",
    )(x, w_in.astype(BF16), ln_g.reshape(1, d), ln_b.reshape(1, d))


def _fnet_dft_kernel(x_ref, z_ref, tc_ref, ts_ref, cdft_ref, w_out_ref,
                     g_ref, b_ref, o_ref, p_scr, q_scr):
    mi = pl.program_id(0)
    qi = pl.program_id(1)
    n_macro = pl.num_programs(0) - 1
    rows = DFT_QSTEP * DFT_TILE

    @pl.when(mi == 0)
    def _():
        o_ref[0] = jnp.zeros(o_ref.shape[1:], F32)

    @pl.when(mi > 0)
    def _():
        slot = (mi - 1) % 2
        quads = pl.ds(qi * DFT_QSTEP, DFT_QSTEP)
        x = x_ref[0].reshape(rows, x_ref.shape[-1])
        p = p_scr[slot, quads].reshape(rows, p_scr.shape[-1])
        q = q_scr[slot, quads].reshape(rows, q_scr.shape[-1])
        gd = p.shape[1] // B_GROUPS
        f_groups = []
        for gi in range(B_GROUPS):
            cols = slice(gi * gd, (gi + 1) * gd)
            pq = jnp.concatenate([p[:, cols], q[:, cols]], axis=1)
            f_groups.append(_dot(pq, cdft_ref[...]))
        f = jnp.concatenate(f_groups, axis=1)
        y = _dot(f.astype(BF16), w_out_ref[...])
        out = _ln(DEEPNORM_ALPHA * x + y, g_ref[...], b_ref[...])
        o_ref[0] = out.reshape(o_ref.shape[1:])

    @pl.when((qi == pl.num_programs(1) - 1) & (mi < n_macro))
    def _():
        slot = mi % 2
        tc = {r: _dot(tc_ref[r], z_ref[0, r]) for r in (0, 2, 1, 3)}
        ts = {r: _dot(ts_ref[r], z_ref[0, r]) for r in (0, 2, 1, 3)}
        c_ev_p, c_ev_m = tc[0] + tc[2], tc[0] - tc[2]
        c_od_p, c_od_m = tc[1] + tc[3], tc[1] - tc[3]
        s_ev_p, s_ev_m = ts[0] + ts[2], ts[0] - ts[2]
        s_od_p, s_od_m = ts[1] + ts[3], ts[1] - ts[3]
        p_scr[slot, 0] = (c_ev_p + c_od_p).astype(BF16)
        q_scr[slot, 0] = (s_ev_p + s_od_p).astype(BF16)
        p_scr[slot, 1] = (c_ev_m - s_od_m).astype(BF16)
        q_scr[slot, 1] = (s_ev_m + c_od_m).astype(BF16)
        p_scr[slot, 2] = (c_ev_p - c_od_p).astype(BF16)
        q_scr[slot, 2] = (s_ev_p - s_od_p).astype(BF16)
        p_scr[slot, 3] = (c_ev_m + s_od_m).astype(BF16)
        q_scr[slot, 3] = (s_ev_m - c_od_m).astype(BF16)


def _cos_sin(index, period):
    ang = (index % period).astype(np.float64) * (2.0 * math.pi / period)
    return np.cos(ang), np.sin(ang)


def _position_dft_tables(s):
    sq = s // DFT_RADIX
    k = np.arange(sq, dtype=np.int64)[None, :, None]
    sp = np.arange(sq, dtype=np.int64)[None, None, :]
    r = np.arange(DFT_RADIX, dtype=np.int64)[:, None, None]
    return _cos_sin(k * (DFT_RADIX * sp + r), s)


def _channel_dft_table(gd, norm):
    j = np.arange(gd, dtype=np.int64)
    cos_c, sin_c = _cos_sin(j[:, None] * j[None, :], gd)
    return np.concatenate([cos_c, -sin_c], axis=0) * norm


def _fnet_dft(x, z, w_out, g, b):
    bsz, s, d = x.shape
    gd = d // B_GROUPS
    sq = s // DFT_RADIX
    tc, ts = _position_dft_tables(s)
    cdft = _channel_dft_table(gd, 1.0 / math.sqrt(s * gd))
    x4 = x.reshape(bsz, DFT_RADIX, sq, d)
    n_kt = sq // DFT_TILE
    n_macro = bsz * n_kt

    def finishing(mi):
        return jnp.maximum(mi - 1, 0)

    def transforming(mi):
        return jnp.minimum(mi, n_macro - 1)

    tile = pl.BlockSpec(
        (1, DFT_QSTEP, DFT_TILE, d),
        lambda mi, qi: (finishing(mi) % bsz, qi, finishing(mi) // bsz, 0))
    seq = pl.BlockSpec((1, DFT_RADIX, sq, d),
                       lambda mi, qi: (transforming(mi) % bsz, 0, 0, 0))
    table = pl.BlockSpec((DFT_RADIX, DFT_TILE, sq),
                         lambda mi, qi: (0, transforming(mi) // bsz, 0),
                         pipeline_mode=pl.Buffered(1))
    out = pl.pallas_call(
        _fnet_dft_kernel,
        grid=(n_macro + 1, DFT_RADIX // DFT_QSTEP),
        in_specs=[tile, seq, table, table, _const_spec((2 * gd, gd)),
                  _const_spec((d, d)), _const_spec((1, d)), _const_spec((1, d))],
        out_specs=tile,
        out_shape=jax.ShapeDtypeStruct((bsz, DFT_RADIX, sq, d), F32),
        scratch_shapes=[pltpu.VMEM((2, DFT_RADIX, DFT_TILE, d), BF16),
                        pltpu.VMEM((2, DFT_RADIX, DFT_TILE, d), BF16)],
        compiler_params=_params(2),
        name="fnet_dft",
    )(x4, z, jnp.asarray(tc, BF16), jnp.asarray(ts, BF16),
      jnp.asarray(cdft, BF16), w_out.astype(BF16),
      g.reshape(1, d), b.reshape(1, d))
    return out.reshape(bsz, s, d)


def _fnet_mixer(x, w_in, ln_g, ln_b, w_out, g, b):
    d = x.shape[-1]
    z = _fnet_in(x, w_in, ln_g.reshape(d), ln_b.reshape(d))
    return _fnet_dft(x, z, w_out, g, b)


def _pool_kernel(x_ref, prev_ref, next_ref, w_in_ref, w_grp_ref, scale_ref,
                 w_out_ref, g_ref, b_ref, o_ref, z_scr, quad_scr, *, seq_len):
    ti = pl.program_id(1)
    tm = x_ref.shape[1]
    x = x_ref[0]
    first = ti == 0
    last = ti == pl.num_programs(1) - 1
    x_prev = jnp.where(first, 0.0, prev_ref[0])
    x_next = jnp.where(last, 0.0, next_ref[0])
    x_all = jnp.concatenate([x_prev, x, x_next], axis=0)
    z_scr[...] = _dot(x_all.astype(BF16), w_in_ref[...])

    t = ti * tm + lax.broadcasted_iota(jnp.int32, (tm, 1), 0)
    gd = x.shape[1] // len(POOL_WINDOWS)
    wide = [gi for gi, w in enumerate(POOL_WINDOWS) if w >= 2 * POOL_QUAD]
    wide_col0 = wide[0] * gd
    max_half = max(POOL_WINDOWS) // 2
    quad_row0 = POOL_HALO - max_half
    quad_rows = tm + 2 * max_half
    quad = z_scr[quad_row0:quad_row0 + quad_rows, wide_col0:]
    for k in range(1, POOL_QUAD):
        quad = quad + z_scr[quad_row0 + k:quad_row0 + k + quad_rows, wide_col0:]
    quad_scr[quad_row0:quad_row0 + quad_rows, :] = quad

    mixed_groups = []
    for gi, w in enumerate(POOL_WINDOWS):
        cols = slice(gi * gd, (gi + 1) * gd)
        half = w // 2
        row0 = POOL_HALO - half
        if gi in wide:
            qcols = slice(gi * gd - wide_col0, (gi + 1) * gd - wide_col0)
            win = quad_scr[row0:row0 + tm, qcols]
            for m in range(1, w // POOL_QUAD):
                r = row0 + m * POOL_QUAD
                win = win + quad_scr[r:r + tm, qcols]
        else:
            win = z_scr[row0:row0 + tm, cols]
            for k in range(1, w):
                win = win + z_scr[row0 + k:row0 + k + tm, cols]
        lo = jnp.maximum(t - half, 0)
        hi = jnp.minimum(t - half + w, seq_len)
        inv_count = 1.0 / (hi - lo).astype(F32)
        pooled = win * inv_count - z_scr[POOL_HALO:POOL_HALO + tm, cols]
        mixed_g = _dot(pooled.astype(BF16), w_grp_ref[gi]) * scale_ref[:, cols]
        mixed_groups.append(mixed_g.astype(BF16))
    mixed = jnp.concatenate(mixed_groups, axis=1)
    for rb in range(tm // POOL_ROWS):
        rows = slice(rb * POOL_ROWS, (rb + 1) * POOL_ROWS)
        y = _dot(mixed[rows], w_out_ref[...])
        o_ref[0, rows, :] = _ln(DEEPNORM_ALPHA * x[rows] + y, g_ref[...],
                                b_ref[...])


def _pool_mixer(x, w_in, w_grp, scale, w_out, g, b):
    bsz, s, d = x.shape
    n_g = len(POOL_WINDOWS)
    gd = d // n_g
    tm = POOL_TILE
    n_narrow = sum(w < 2 * POOL_QUAD for w in POOL_WINDOWS)
    per_tile = tm // POOL_HALO
    n_halo = s // POOL_HALO
    tile = pl.BlockSpec((1, tm, d), lambda bi, ti: (bi, ti, 0))
    prev = pl.BlockSpec(
        (1, POOL_HALO, d),
        lambda bi, ti: (bi, jnp.maximum(ti * per_tile - 1, 0), 0))
    nxt = pl.BlockSpec(
        (1, POOL_HALO, d),
        lambda bi, ti: (bi, jnp.minimum((ti + 1) * per_tile, n_halo - 1), 0))
    return pl.pallas_call(
        functools.partial(_pool_kernel, seq_len=s),
        grid=(bsz, s // tm),
        in_specs=[tile, prev, nxt, _const_spec((d, d)),
                  _const_spec((n_g, gd, gd)), _const_spec((1, d)),
                  _const_spec((d, d)), _const_spec((1, d)), _const_spec((1, d))],
        out_specs=tile,
        out_shape=jax.ShapeDtypeStruct((bsz, s, d), F32),
        scratch_shapes=[
            pltpu.VMEM((tm + 2 * POOL_HALO, d), F32),
            pltpu.VMEM((tm + 2 * POOL_HALO, d - n_narrow * gd), F32)],
        compiler_params=_params(2),
        name="pool_mixer",
    )(x, x, x, w_in.astype(BF16), w_grp.astype(BF16), scale.reshape(1, d),
      w_out.astype(BF16), g.reshape(1, d), b.reshape(1, d))


def kernel(x, ln1_g, ln1_b, ffn_w1, ffn_b1, ffn_w2, ffn_b2, ln2_g, ln2_b,
           a_w_in, a_ln_g, a_ln_b, a_w_s, a_b_s, a_w_out,
           b_w_in, b_ln_g, b_ln_b, b_w_out,
           c_w_in, c_w_grp, c_scale, c_w_out):
    bsz, s, d = x.shape
    for i in range(DEPTH):
        kind, j = i % N_MIXERS, i // N_MIXERS
        if kind == 0:
            x = _gmlp_mixer(x.reshape(bsz * s, d), a_w_in[j], a_ln_g[j],
                            a_ln_b[j], a_w_s[j], a_b_s[j], a_w_out[j],
                            ln1_g[i], ln1_b[i])
        elif kind == 1:
            x = _fnet_mixer(x.reshape(bsz, s, d), b_w_in[j], b_ln_g[j],
                            b_ln_b[j], b_w_out[j], ln1_g[i], ln1_b[i])
        else:
            x = _pool_mixer(x.reshape(bsz, s, d), c_w_in[j], c_w_grp[j],
                            c_scale[j], c_w_out[j], ln1_g[i], ln1_b[i])
        x = _ffn(x.reshape(bsz * s, d), ffn_w1[i], ffn_b1[i], ffn_w2[i],
                 ffn_b2[i], ln2_g[i], ln2_b[i])
    return x.reshape(bsz, s, d)
```

```python
import functools
import math

import jax
import jax.numpy as jnp
import numpy as np
from jax import lax
from jax.experimental import pallas as pl
from jax.experimental.pallas import tpu as pltpu

DEPTH = 4
N_MIXERS = 3
DEEPNORM_ALPHA = (2.0 * DEPTH) ** 0.25
LN_EPS = 1e-5

A_CHUNK = 128
A_GROUPS = 8
B_GROUPS = 4
POOL_WINDOWS = (2, 4, 8, 16)
POOL_ROWS = 128
LANES = 128
POOL_HALO = 16
POOL_QUAD = 4

V7X_VMEM_LIMIT_BYTES = 56 * 1024 * 1024
POOL_TILE = 1024
GMLP_TILE = 1024
GMLP_COLS = 256
FFN_TILE = 1024
FFN_ROWS = 256
FFN_CHUNK = 1024
DFT_RADIX = 4
DFT_TILE = 256
DFT_QSTEP = 2
FNET_IN_TILE = 256

BF16 = jnp.bfloat16
F32 = jnp.float32


def _ln(r, g, b):
    mu = jnp.mean(r, axis=-1, keepdims=True)
    c = r - mu
    var = jnp.mean(c * c, axis=-1, keepdims=True)
    return c * lax.rsqrt(var + LN_EPS) * g + b


def _gelu_tanh(x):
    c = math.sqrt(2.0 / math.pi)
    half = 0.5 * x
    return half + half * jnp.tanh(x * (c + (c * 0.044715) * (x * x)))


def _dot(a, b):
    return jnp.dot(a, b, preferred_element_type=F32)


def _const_spec(shape):
    zeros = (0,) * len(shape)
    return pl.BlockSpec(shape, lambda *_: zeros, pipeline_mode=pl.Buffered(1))


def _params(n_axes):
    return pltpu.CompilerParams(
        dimension_semantics=("arbitrary",) * n_axes,
        vmem_limit_bytes=V7X_VMEM_LIMIT_BYTES)


def _ffn_kernel(x_ref, w1_ref, b1_ref, w2_ref, b2_ref, g_ref, b_ref, o_ref):
    d_ff = w1_ref.shape[1]
    n_chunks = d_ff // FFN_CHUNK
    x = x_ref[...]
    xb = x.astype(BF16)
    acc = jnp.zeros(x.shape, F32)
    for c in range(n_chunks):
        cols = slice(c * FFN_CHUNK, (c + 1) * FFN_CHUNK)
        h = _dot(xb, w1_ref[:, cols]) + b1_ref[:, cols]
        h = jnp.square(jnp.maximum(h, 0.0)).astype(BF16)
        if c < n_chunks - 1:
            acc = acc + _dot(h, w2_ref[cols, :])
            continue
        for rb in range(x.shape[0] // FFN_ROWS):
            rows = slice(rb * FFN_ROWS, (rb + 1) * FFN_ROWS)
            y = acc[rows] + _dot(h[rows], w2_ref[cols, :]) + b2_ref[...]
            o_ref[rows, :] = _ln(DEEPNORM_ALPHA * x[rows] + y, g_ref[...],
                                 b_ref[...])


def _ffn(x2d, w1, b1, w2, b2, g, b):
    n, d = x2d.shape
    f = w1.shape[1]
    row = pl.BlockSpec((FFN_TILE, d), lambda i: (i, 0))
    return pl.pallas_call(
        _ffn_kernel,
        grid=(n // FFN_TILE,),
        in_specs=[row, _const_spec((d, f)), _const_spec((1, f)),
                  _const_spec((f, d)), _const_spec((1, d)),
                  _const_spec((1, d)), _const_spec((1, d))],
        out_specs=row,
        out_shape=jax.ShapeDtypeStruct((n, d), F32),
        compiler_params=_params(1),
        name="ffn",
    )(x2d, w1.astype(BF16), b1.reshape(1, f), w2.astype(BF16),
      b2.reshape(1, d), g.reshape(1, d), b.reshape(1, d))


def _gmlp_kernel(x_ref, w_in_ref, lng_ref, lnb_ref, ws_ref, bs_ref, w_out_ref,
                 g_ref, b_ref, o_ref):
    x = x_ref[...]
    xb = x.astype(BF16)
    width = w_out_ref.shape[0]
    gd = width // A_GROUPS
    n_chunks = x.shape[0] // A_CHUNK

    def proj(col0):
        return jnp.concatenate(
            [_gelu_tanh(_dot(xb, w_in_ref[:, c:c + GMLP_COLS]))
             for c in range(col0, col0 + width, GMLP_COLS)], axis=1)

    v = _ln(proj(width), lng_ref[...], lnb_ref[...]).astype(BF16)
    u = proj(0)
    mixed_groups = []
    for gi in range(A_GROUPS):
        cols = slice(gi * gd, (gi + 1) * gd)
        v_g = jnp.concatenate(
            [v[n * A_CHUNK:(n + 1) * A_CHUNK, cols] for n in range(n_chunks)],
            axis=1)
        mixed_groups.append(_dot(ws_ref[gi], v_g))
    for n in range(n_chunks):
        rows = slice(n * A_CHUNK, (n + 1) * A_CHUNK)
        mixed = jnp.concatenate(
            [m[:, n * gd:(n + 1) * gd] for m in mixed_groups], axis=1)
        out = u[rows] * (mixed + bs_ref[...])
        y = _dot(out.astype(BF16), w_out_ref[...])
        o_ref[rows, :] = _ln(DEEPNORM_ALPHA * x[rows] + y, g_ref[...], b_ref[...])


def _gmlp_mixer(x2d, w_in, ln_g, ln_b, w_s, b_s, w_out, g, b):
    n, d = x2d.shape
    width = w_out.shape[0]
    gd = width // A_GROUPS
    bias_map = jnp.repeat(b_s.T, gd, axis=1)
    row = pl.BlockSpec((GMLP_TILE, d), lambda i: (i, 0))
    return pl.pallas_call(
        _gmlp_kernel,
        grid=(n // GMLP_TILE,),
        in_specs=[row, _const_spec((d, 2 * width)), _const_spec((1, width)),
                  _const_spec((1, width)),
                  _const_spec((A_GROUPS, A_CHUNK, A_CHUNK)),
                  _const_spec((A_CHUNK, width)), _const_spec((width, d)),
                  _const_spec((1, d)), _const_spec((1, d))],
        out_specs=row,
        out_shape=jax.ShapeDtypeStruct((n, d), F32),
        compiler_params=_params(1),
        name="gmlp_mixer",
    )(x2d, w_in.astype(BF16), ln_g.reshape(1, width), ln_b.reshape(1, width),
      w_s.astype(BF16), bias_map, w_out.astype(BF16),
      g.reshape(1, d), b.reshape(1, d))


def _fnet_in_kernel(x_ref, w_in_ref, lng_ref, lnb_ref, z_ref, x_scr):
    d = w_in_ref.shape[0]
    rows = x_ref.shape[0] // DFT_RADIX
    n_lane_blocks = d // LANES
    for c in range(n_lane_blocks):
        x_scr[c] = x_ref[:, c * LANES:(c + 1) * LANES]
    xs = jnp.concatenate(
        [jnp.concatenate(
            [x_scr[pl.ds(c, 1), pl.ds(r, rows, stride=DFT_RADIX), :][0]
             for c in range(n_lane_blocks)], axis=1)
         for r in range(DFT_RADIX)], axis=0)
    z = _dot(xs.astype(BF16), w_in_ref[...])
    gd = d // B_GROUPS
    for gi in range(B_GROUPS):
        cols = slice(gi * gd, (gi + 1) * gd)
        zn = _ln(z[:, cols], lng_ref[:, cols], lnb_ref[:, cols]).astype(BF16)
        for r in range(DFT_RADIX):
            z_ref[0, r, :, cols] = zn[r * rows:(r + 1) * rows]


def _fnet_in(x, w_in, ln_g, ln_b):
    bsz, s, d = x.shape
    sq = s // DFT_RADIX
    return pl.pallas_call(
        _fnet_in_kernel,
        grid=(bsz, sq // FNET_IN_TILE),
        in_specs=[pl.BlockSpec((None, DFT_RADIX * FNET_IN_TILE, d),
                               lambda bi, ti: (bi, ti, 0)),
                  _const_spec((d, d)), _const_spec((1, d)), _const_spec((1, d))],
        out_specs=pl.BlockSpec((1, DFT_RADIX, FNET_IN_TILE, d),
                               lambda bi, ti: (bi, 0, ti, 0)),
        out_shape=jax.ShapeDtypeStruct((bsz, DFT_RADIX, sq, d), BF16),
        scratch_shapes=[pltpu.VMEM(
            (d // LANES, DFT_RADIX * FNET_IN_TILE, LANES), F32)],
        compiler_params=_params(2),
        name="fnet_in",
    )(x, w_in.astype(BF16), ln_g.reshape(1, d), ln_b.reshape(1, d))


def _fnet_dft_kernel(x_ref, z_ref, tc_ref, ts_ref, cdft_ref, w_out_ref,
                     g_ref, b_ref, o_ref, p_scr, q_scr):
    step = pl.program_id(0)
    per_macro = DFT_RADIX // DFT_QSTEP
    n_macro = (pl.num_programs(0) - 1) // per_macro
    mi = jnp.maximum(step - 1, 0) // per_macro
    qi = jnp.maximum(step - 1, 0) % per_macro
    rows = DFT_QSTEP * DFT_TILE

    @pl.when(step > 0)
    def _():
        slot = mi % 2
        quads = pl.ds(qi * DFT_QSTEP, DFT_QSTEP)
        x = x_ref[0].reshape(rows, x_ref.shape[-1])
        p = p_scr[slot, quads].reshape(rows, p_scr.shape[-1])
        q = q_scr[slot, quads].reshape(rows, q_scr.shape[-1])
        gd = p.shape[1] // B_GROUPS
        f_groups = []
        for gi in range(B_GROUPS):
            cols = slice(gi * gd, (gi + 1) * gd)
            pq = jnp.concatenate([p[:, cols], q[:, cols]], axis=1)
            f_groups.append(_dot(pq, cdft_ref[...]))
        f = jnp.concatenate(f_groups, axis=1)
        y = _dot(f.astype(BF16), w_out_ref[...])
        out = _ln(DEEPNORM_ALPHA * x + y, g_ref[...], b_ref[...])
        o_ref[0] = out.reshape(o_ref.shape[1:])

    starts_macro = jnp.where(step == 0, 0, mi + 1)

    @pl.when((step == 0) | ((qi == per_macro - 1) & (mi + 1 < n_macro)))
    def _():
        slot = starts_macro % 2
        tc = {r: _dot(tc_ref[r], z_ref[0, r]) for r in (0, 2, 1, 3)}
        ts = {r: _dot(ts_ref[r], z_ref[0, r]) for r in (0, 2, 1, 3)}
        c_ev_p, c_ev_m = tc[0] + tc[2], tc[0] - tc[2]
        c_od_p, c_od_m = tc[1] + tc[3], tc[1] - tc[3]
        s_ev_p, s_ev_m = ts[0] + ts[2], ts[0] - ts[2]
        s_od_p, s_od_m = ts[1] + ts[3], ts[1] - ts[3]
        p_scr[slot, 0] = (c_ev_p + c_od_p).astype(BF16)
        q_scr[slot, 0] = (s_ev_p + s_od_p).astype(BF16)
        p_scr[slot, 1] = (c_ev_m - s_od_m).astype(BF16)
        q_scr[slot, 1] = (s_ev_m + c_od_m).astype(BF16)
        p_scr[slot, 2] = (c_ev_p - c_od_p).astype(BF16)
        q_scr[slot, 2] = (s_ev_p - s_od_p).astype(BF16)
        p_scr[slot, 3] = (c_ev_m + s_od_m).astype(BF16)
        q_scr[slot, 3] = (s_ev_m - c_od_m).astype(BF16)


def _cos_sin(index, period):
    ang = (index % period).astype(np.float64) * (2.0 * math.pi / period)
    return np.cos(ang), np.sin(ang)


def _position_dft_tables(s):
    sq = s // DFT_RADIX
    k = np.arange(sq, dtype=np.int64)[None, :, None]
    sp = np.arange(sq, dtype=np.int64)[None, None, :]
    r = np.arange(DFT_RADIX, dtype=np.int64)[:, None, None]
    return _cos_sin(k * (DFT_RADIX * sp + r), s)


def _channel_dft_table(gd, norm):
    j = np.arange(gd, dtype=np.int64)
    cos_c, sin_c = _cos_sin(j[:, None] * j[None, :], gd)
    return np.concatenate([cos_c, -sin_c], axis=0) * norm


def _fnet_dft(x, z, w_out, g, b):
    bsz, s, d = x.shape
    gd = d // B_GROUPS
    sq = s // DFT_RADIX
    tc, ts = _position_dft_tables(s)
    cdft = _channel_dft_table(gd, 1.0 / math.sqrt(s * gd))
    x4 = x.reshape(bsz, DFT_RADIX, sq, d)
    n_kt = sq // DFT_TILE
    n_macro = bsz * n_kt

    per_macro = DFT_RADIX // DFT_QSTEP

    def finishing(step):
        return jnp.maximum(step - 1, 0) // per_macro

    def transforming(step):
        return jnp.where(step == 0, 0,
                         jnp.minimum(finishing(step) + 1, n_macro - 1))

    tile = pl.BlockSpec(
        (1, DFT_QSTEP, DFT_TILE, d),
        lambda step: (finishing(step) % bsz,
                      jnp.maximum(step - 1, 0) % per_macro,
                      finishing(step) // bsz, 0))
    seq = pl.BlockSpec((1, DFT_RADIX, sq, d),
                       lambda step: (transforming(step) % bsz, 0, 0, 0))
    table = pl.BlockSpec((DFT_RADIX, DFT_TILE, sq),
                         lambda step: (0, transforming(step) // bsz, 0),
                         pipeline_mode=pl.Buffered(1))
    out = pl.pallas_call(
        _fnet_dft_kernel,
        grid=(1 + n_macro * per_macro,),
        in_specs=[tile, seq, table, table, _const_spec((2 * gd, gd)),
                  _const_spec((d, d)), _const_spec((1, d)), _const_spec((1, d))],
        out_specs=tile,
        out_shape=jax.ShapeDtypeStruct((bsz, DFT_RADIX, sq, d), F32),
        scratch_shapes=[pltpu.VMEM((2, DFT_RADIX, DFT_TILE, d), BF16),
                        pltpu.VMEM((2, DFT_RADIX, DFT_TILE, d), BF16)],
        compiler_params=_params(1),
        name="fnet_dft",
    )(x4, z, jnp.asarray(tc, BF16), jnp.asarray(ts, BF16),
      jnp.asarray(cdft, BF16), w_out.astype(BF16),
      g.reshape(1, d), b.reshape(1, d))
    return out.reshape(bsz, s, d)


def _fnet_mixer(x, w_in, ln_g, ln_b, w_out, g, b):
    d = x.shape[-1]
    z = _fnet_in(x, w_in, ln_g.reshape(d), ln_b.reshape(d))
    return _fnet_dft(x, z, w_out, g, b)


def _pool_kernel(x_ref, prev_ref, next_ref, w_in_ref, w_grp_ref, scale_ref,
                 w_out_ref, g_ref, b_ref, o_ref, z_scr, quad_scr, *, seq_len):
    ti = pl.program_id(1)
    tm = x_ref.shape[1]
    x = x_ref[0]
    first = ti == 0
    last = ti == pl.num_programs(1) - 1
    x_prev = jnp.where(first, 0.0, prev_ref[0])
    x_next = jnp.where(last, 0.0, next_ref[0])
    x_all = jnp.concatenate([x_prev, x, x_next], axis=0)
    z_scr[...] = _dot(x_all.astype(BF16), w_in_ref[...])

    t = ti * tm + lax.broadcasted_iota(jnp.int32, (tm, 1), 0)
    gd = x.shape[1] // len(POOL_WINDOWS)
    wide = [gi for gi, w in enumerate(POOL_WINDOWS) if w >= 2 * POOL_QUAD]
    wide_col0 = wide[0] * gd
    max_half = max(POOL_WINDOWS) // 2
    quad_row0 = POOL_HALO - max_half
    quad_rows = tm + 2 * max_half
    quad = z_scr[quad_row0:quad_row0 + quad_rows, wide_col0:]
    for k in range(1, POOL_QUAD):
        quad = quad + z_scr[quad_row0 + k:quad_row0 + k + quad_rows, wide_col0:]
    quad_scr[quad_row0:quad_row0 + quad_rows, :] = quad

    mixed_groups = []
    for gi, w in enumerate(POOL_WINDOWS):
        cols = slice(gi * gd, (gi + 1) * gd)
        half = w // 2
        row0 = POOL_HALO - half
        if gi in wide:
            qcols = slice(gi * gd - wide_col0, (gi + 1) * gd - wide_col0)
            win = quad_scr[row0:row0 + tm, qcols]
            for m in range(1, w // POOL_QUAD):
                r = row0 + m * POOL_QUAD
                win = win + quad_scr[r:r + tm, qcols]
        else:
            win = z_scr[row0:row0 + tm, cols]
            for k in range(1, w):
                win = win + z_scr[row0 + k:row0 + k + tm, cols]
        lo = jnp.maximum(t - half, 0)
        hi = jnp.minimum(t - half + w, seq_len)
        inv_count = 1.0 / (hi - lo).astype(F32)
        pooled = win * inv_count - z_scr[POOL_HALO:POOL_HALO + tm, cols]
        mixed_g = _dot(pooled.astype(BF16), w_grp_ref[gi]) * scale_ref[:, cols]
        mixed_groups.append(mixed_g.astype(BF16))
    mixed = jnp.concatenate(mixed_groups, axis=1)
    for rb in range(tm // POOL_ROWS):
        rows = slice(rb * POOL_ROWS, (rb + 1) * POOL_ROWS)
        y = _dot(mixed[rows], w_out_ref[...])
        o_ref[0, rows, :] = _ln(DEEPNORM_ALPHA * x[rows] + y, g_ref[...],
                                b_ref[...])


def _pool_mixer(x, w_in, w_grp, scale, w_out, g, b):
    bsz, s, d = x.shape
    n_g = len(POOL_WINDOWS)
    gd = d // n_g
    tm = POOL_TILE
    n_narrow = sum(w < 2 * POOL_QUAD for w in POOL_WINDOWS)
    per_tile = tm // POOL_HALO
    n_halo = s // POOL_HALO
    tile = pl.BlockSpec((1, tm, d), lambda bi, ti: (bi, ti, 0))
    prev = pl.BlockSpec(
        (1, POOL_HALO, d),
        lambda bi, ti: (bi, jnp.maximum(ti * per_tile - 1, 0), 0))
    nxt = pl.BlockSpec(
        (1, POOL_HALO, d),
        lambda bi, ti: (bi, jnp.minimum((ti + 1) * per_tile, n_halo - 1), 0))
    return pl.pallas_call(
        functools.partial(_pool_kernel, seq_len=s),
        grid=(bsz, s // tm),
        in_specs=[tile, prev, nxt, _const_spec((d, d)),
                  _const_spec((n_g, gd, gd)), _const_spec((1, d)),
                  _const_spec((d, d)), _const_spec((1, d)), _const_spec((1, d))],
        out_specs=tile,
        out_shape=jax.ShapeDtypeStruct((bsz, s, d), F32),
        scratch_shapes=[
            pltpu.VMEM((tm + 2 * POOL_HALO, d), F32),
            pltpu.VMEM((tm + 2 * POOL_HALO, d - n_narrow * gd), F32)],
        compiler_params=_params(2),
        name="pool_mixer",
    )(x, x, x, w_in.astype(BF16), w_grp.astype(BF16), scale.reshape(1, d),
      w_out.astype(BF16), g.reshape(1, d), b.reshape(1, d))


def kernel(x, ln1_g, ln1_b, ffn_w1, ffn_b1, ffn_w2, ffn_b2, ln2_g, ln2_b,
           a_w_in, a_ln_g, a_ln_b, a_w_s, a_b_s, a_w_out,
           b_w_in, b_ln_g, b_ln_b, b_w_out,
           c_w_in, c_w_grp, c_scale, c_w_out):
    bsz, s, d = x.shape
    for i in range(DEPTH):
        kind, j = i % N_MIXERS, i // N_MIXERS
        if kind == 0:
            x = _gmlp_mixer(x.reshape(bsz * s, d), a_w_in[j], a_ln_g[j],
                            a_ln_b[j], a_w_s[j], a_b_s[j], a_w_out[j],
                            ln1_g[i], ln1_b[i])
        elif kind == 1:
            x = _fnet_mixer(x.reshape(bsz, s, d), b_w_in[j], b_ln_g[j],
                            b_ln_b[j], b_w_out[j], ln1_g[i], ln1_b[i])
        else:
            x = _pool_mixer(x.reshape(bsz, s, d), c_w_in[j], c_w_grp[j],
                            c_scale[j], c_w_out[j], ln1_g[i], ln1_b[i])
        x = _ffn(x.reshape(bsz * s, d), ffn_w1[i], ffn_b1[i], ffn_w2[i],
                 ffn_b2[i], ln2_g[i], ln2_b[i])
    return x.reshape(bsz, s, d)
```

```python
import functools
import math

import jax
import jax.numpy as jnp
import numpy as np
from jax import lax
from jax.experimental import pallas as pl
from jax.experimental.pallas import tpu as pltpu

DEPTH = 4
N_MIXERS = 3
DEEPNORM_ALPHA = (2.0 * DEPTH) ** 0.25
LN_EPS = 1e-5

A_CHUNK = 128
A_GROUPS = 8
B_GROUPS = 4
POOL_WINDOWS = (2, 4, 8, 16)
POOL_ROWS = 256
LANES = 128
POOL_HALO = 16
POOL_QUAD = 4

V7X_VMEM_LIMIT_BYTES = 56 * 1024 * 1024
POOL_TILE = 1024
GMLP_TILE = 1024
GMLP_ROWS = 256
GMLP_COLS = 256
FFN_TILE = 1024
FFN_ROWS = 256
FFN_CHUNK = 1024
DFT_RADIX = 4
DFT_TILE = 256
DFT_QSTEP = 2
FNET_IN_TILE = 256

BF16 = jnp.bfloat16
F32 = jnp.float32


def _ln(r, g, b):
    mu = jnp.mean(r, axis=-1, keepdims=True)
    c = r - mu
    var = jnp.mean(c * c, axis=-1, keepdims=True)
    return c * lax.rsqrt(var + LN_EPS) * g + b


def _gelu_tanh(x):
    c = math.sqrt(2.0 / math.pi)
    half = 0.5 * x
    return half + half * jnp.tanh(x * (c + (c * 0.044715) * (x * x)))


def _dot(a, b):
    return jnp.dot(a, b, preferred_element_type=F32)


def _const_spec(shape):
    zeros = (0,) * len(shape)
    return pl.BlockSpec(shape, lambda *_: zeros, pipeline_mode=pl.Buffered(1))


def _params(n_axes):
    return pltpu.CompilerParams(
        dimension_semantics=("arbitrary",) * n_axes,
        vmem_limit_bytes=V7X_VMEM_LIMIT_BYTES)


def _ffn_kernel(x_ref, w1_ref, b1_ref, w2_ref, b2_ref, g_ref, b_ref, o_ref):
    d_ff = w1_ref.shape[1]
    n_chunks = d_ff // FFN_CHUNK
    x = x_ref[...]
    xb = x.astype(BF16)
    acc = jnp.zeros(x.shape, F32)
    for c in range(n_chunks):
        cols = slice(c * FFN_CHUNK, (c + 1) * FFN_CHUNK)
        h = _dot(xb, w1_ref[:, cols]) + b1_ref[:, cols]
        h = jnp.square(jnp.maximum(h, 0.0)).astype(BF16)
        if c < n_chunks - 1:
            acc = acc + _dot(h, w2_ref[cols, :])
            continue
        for rb in range(x.shape[0] // FFN_ROWS):
            rows = slice(rb * FFN_ROWS, (rb + 1) * FFN_ROWS)
            y = acc[rows] + _dot(h[rows], w2_ref[cols, :]) + b2_ref[...]
            o_ref[rows, :] = _ln(DEEPNORM_ALPHA * x[rows] + y, g_ref[...],
                                 b_ref[...])


def _ffn(x2d, w1, b1, w2, b2, g, b):
    n, d = x2d.shape
    f = w1.shape[1]
    row = pl.BlockSpec((FFN_TILE, d), lambda i: (i, 0))
    return pl.pallas_call(
        _ffn_kernel,
        grid=(n // FFN_TILE,),
        in_specs=[row, _const_spec((d, f)), _const_spec((1, f)),
                  _const_spec((f, d)), _const_spec((1, d)),
                  _const_spec((1, d)), _const_spec((1, d))],
        out_specs=row,
        out_shape=jax.ShapeDtypeStruct((n, d), F32),
        compiler_params=_params(1),
        name="ffn",
    )(x2d, w1.astype(BF16), b1.reshape(1, f), w2.astype(BF16),
      b2.reshape(1, d), g.reshape(1, d), b.reshape(1, d))


def _gmlp_kernel(x_ref, w_in_ref, lng_ref, lnb_ref, ws_ref, bs_ref, w_out_ref,
                 g_ref, b_ref, o_ref):
    x = x_ref[...]
    xb = x.astype(BF16)
    width = w_out_ref.shape[0]
    gd = width // A_GROUPS
    n_chunks = x.shape[0] // A_CHUNK

    def proj(col0):
        return jnp.concatenate(
            [_gelu_tanh(_dot(xb, w_in_ref[:, c:c + GMLP_COLS]))
             for c in range(col0, col0 + width, GMLP_COLS)], axis=1)

    v = _ln(proj(width), lng_ref[...], lnb_ref[...]).astype(BF16)
    u = proj(0)
    mixed_groups = []
    for gi in range(A_GROUPS):
        cols = slice(gi * gd, (gi + 1) * gd)
        v_g = jnp.concatenate(
            [v[n * A_CHUNK:(n + 1) * A_CHUNK, cols] for n in range(n_chunks)],
            axis=1)
        mixed_groups.append(_dot(ws_ref[gi], v_g))
    per_block = GMLP_ROWS // A_CHUNK
    for blk in range(n_chunks // per_block):
        rows = slice(blk * GMLP_ROWS, (blk + 1) * GMLP_ROWS)
        mixed = jnp.concatenate(
            [jnp.concatenate(
                [m[:, n * gd:(n + 1) * gd] for m in mixed_groups], axis=1)
             + bs_ref[...]
             for n in range(blk * per_block, (blk + 1) * per_block)], axis=0)
        out = u[rows] * mixed
        y = _dot(out.astype(BF16), w_out_ref[...])
        o_ref[rows, :] = _ln(DEEPNORM_ALPHA * x[rows] + y, g_ref[...], b_ref[...])


def _gmlp_mixer(x2d, w_in, ln_g, ln_b, w_s, b_s, w_out, g, b):
    n, d = x2d.shape
    width = w_out.shape[0]
    gd = width // A_GROUPS
    bias_map = jnp.repeat(b_s.T, gd, axis=1)
    row = pl.BlockSpec((GMLP_TILE, d), lambda i: (i, 0))
    return pl.pallas_call(
        _gmlp_kernel,
        grid=(n // GMLP_TILE,),
        in_specs=[row, _const_spec((d, 2 * width)), _const_spec((1, width)),
                  _const_spec((1, width)),
                  _const_spec((A_GROUPS, A_CHUNK, A_CHUNK)),
                  _const_spec((A_CHUNK, width)), _const_spec((width, d)),
                  _const_spec((1, d)), _const_spec((1, d))],
        out_specs=row,
        out_shape=jax.ShapeDtypeStruct((n, d), F32),
        compiler_params=_params(1),
        name="gmlp_mixer",
    )(x2d, w_in.astype(BF16), ln_g.reshape(1, width), ln_b.reshape(1, width),
      w_s.astype(BF16), bias_map, w_out.astype(BF16),
      g.reshape(1, d), b.reshape(1, d))


def _fnet_in_kernel(x_ref, w_in_ref, lng_ref, lnb_ref, z_ref, x_scr):
    d = w_in_ref.shape[0]
    rows = x_ref.shape[0] // DFT_RADIX
    n_lane_blocks = d // LANES
    for c in range(n_lane_blocks):
        x_scr[c] = x_ref[:, c * LANES:(c + 1) * LANES]
    xs = jnp.concatenate(
        [jnp.concatenate(
            [x_scr[pl.ds(c, 1), pl.ds(r, rows, stride=DFT_RADIX), :][0]
             for c in range(n_lane_blocks)], axis=1)
         for r in range(DFT_RADIX)], axis=0)
    z = _dot(xs.astype(BF16), w_in_ref[...])
    gd = d // B_GROUPS
    for gi in range(B_GROUPS):
        cols = slice(gi * gd, (gi + 1) * gd)
        zn = _ln(z[:, cols], lng_ref[:, cols], lnb_ref[:, cols]).astype(BF16)
        for r in range(DFT_RADIX):
            z_ref[0, r, :, cols] = zn[r * rows:(r + 1) * rows]


def _fnet_in(x, w_in, ln_g, ln_b):
    bsz, s, d = x.shape
    sq = s // DFT_RADIX
    return pl.pallas_call(
        _fnet_in_kernel,
        grid=(bsz, sq // FNET_IN_TILE),
        in_specs=[pl.BlockSpec((None, DFT_RADIX * FNET_IN_TILE, d),
                               lambda bi, ti: (bi, ti, 0)),
                  _const_spec((d, d)), _const_spec((1, d)), _const_spec((1, d))],
        out_specs=pl.BlockSpec((1, DFT_RADIX, FNET_IN_TILE, d),
                               lambda bi, ti: (bi, 0, ti, 0)),
        out_shape=jax.ShapeDtypeStruct((bsz, DFT_RADIX, sq, d), BF16),
        scratch_shapes=[pltpu.VMEM(
            (d // LANES, DFT_RADIX * FNET_IN_TILE, LANES), F32)],
        compiler_params=_params(2),
        name="fnet_in",
    )(x, w_in.astype(BF16), ln_g.reshape(1, d), ln_b.reshape(1, d))


def _fnet_dft_kernel(x_ref, z_ref, tc_ref, ts_ref, cdft_ref, w_out_ref,
                     g_ref, b_ref, o_ref, p_scr, q_scr):
    step = pl.program_id(0)
    per_macro = DFT_RADIX // DFT_QSTEP
    n_macro = (pl.num_programs(0) - 1) // per_macro
    mi = jnp.maximum(step - 1, 0) // per_macro
    qi = jnp.maximum(step - 1, 0) % per_macro
    rows = DFT_QSTEP * DFT_TILE

    @pl.when(step > 0)
    def _():
        slot = mi % 2
        quads = pl.ds(qi * DFT_QSTEP, DFT_QSTEP)
        x = x_ref[0].reshape(rows, x_ref.shape[-1])
        p = p_scr[slot, quads].reshape(rows, p_scr.shape[-1])
        q = q_scr[slot, quads].reshape(rows, q_scr.shape[-1])
        gd = p.shape[1] // B_GROUPS
        f_groups = []
        for gi in range(B_GROUPS):
            cols = slice(gi * gd, (gi + 1) * gd)
            pq = jnp.concatenate([p[:, cols], q[:, cols]], axis=1)
            f_groups.append(_dot(pq, cdft_ref[...]))
        f = jnp.concatenate(f_groups, axis=1).astype(BF16)
        for j in range(DFT_QSTEP):
            blk = slice(j * DFT_TILE, (j + 1) * DFT_TILE)
            y = _dot(f[blk], w_out_ref[...])
            o_ref[0, j] = _ln(DEEPNORM_ALPHA * x[blk] + y, g_ref[...], b_ref[...])

    starts_macro = jnp.where(step == 0, 0, mi + 1)

    @pl.when((step == 0) | ((qi == per_macro - 1) & (mi + 1 < n_macro)))
    def _():
        slot = starts_macro % 2
        tc = {r: _dot(tc_ref[r], z_ref[0, r]) for r in (0, 2, 1, 3)}
        ts = {r: _dot(ts_ref[r], z_ref[0, r]) for r in (0, 2, 1, 3)}
        c_ev_p, c_ev_m = tc[0] + tc[2], tc[0] - tc[2]
        c_od_p, c_od_m = tc[1] + tc[3], tc[1] - tc[3]
        s_ev_p, s_ev_m = ts[0] + ts[2], ts[0] - ts[2]
        s_od_p, s_od_m = ts[1] + ts[3], ts[1] - ts[3]
        p_scr[slot, 0] = (c_ev_p + c_od_p).astype(BF16)
        q_scr[slot, 0] = (s_ev_p + s_od_p).astype(BF16)
        p_scr[slot, 1] = (c_ev_m - s_od_m).astype(BF16)
        q_scr[slot, 1] = (s_ev_m + c_od_m).astype(BF16)
        p_scr[slot, 2] = (c_ev_p - c_od_p).astype(BF16)
        q_scr[slot, 2] = (s_ev_p - s_od_p).astype(BF16)
        p_scr[slot, 3] = (c_ev_m + s_od_m).astype(BF16)
        q_scr[slot, 3] = (s_ev_m - c_od_m).astype(BF16)


def _cos_sin(index, period):
    ang = (index % period).astype(np.float64) * (2.0 * math.pi / period)
    return np.cos(ang), np.sin(ang)


def _position_dft_tables(s):
    sq = s // DFT_RADIX
    k = np.arange(sq, dtype=np.int64)[None, :, None]
    sp = np.arange(sq, dtype=np.int64)[None, None, :]
    r = np.arange(DFT_RADIX, dtype=np.int64)[:, None, None]
    return _cos_sin(k * (DFT_RADIX * sp + r), s)


def _channel_dft_table(gd, norm):
    j = np.arange(gd, dtype=np.int64)
    cos_c, sin_c = _cos_sin(j[:, None] * j[None, :], gd)
    return np.concatenate([cos_c, -sin_c], axis=0) * norm


def _fnet_dft(x, z, w_out, g, b):
    bsz, s, d = x.shape
    gd = d // B_GROUPS
    sq = s // DFT_RADIX
    tc, ts = _position_dft_tables(s)
    cdft = _channel_dft_table(gd, 1.0 / math.sqrt(s * gd))
    x4 = x.reshape(bsz, DFT_RADIX, sq, d)
    n_kt = sq // DFT_TILE
    n_macro = bsz * n_kt

    per_macro = DFT_RADIX // DFT_QSTEP

    def finishing(step):
        return jnp.maximum(step - 1, 0) // per_macro

    def transforming(step):
        return jnp.where(step == 0, 0,
                         jnp.minimum(finishing(step) + 1, n_macro - 1))

    tile = pl.BlockSpec(
        (1, DFT_QSTEP, DFT_TILE, d),
        lambda step: (finishing(step) % bsz,
                      jnp.maximum(step - 1, 0) % per_macro,
                      finishing(step) // bsz, 0))
    seq = pl.BlockSpec((1, DFT_RADIX, sq, d),
                       lambda step: (transforming(step) % bsz, 0, 0, 0))
    table = pl.BlockSpec((DFT_RADIX, DFT_TILE, sq),
                         lambda step: (0, transforming(step) // bsz, 0),
                         pipeline_mode=pl.Buffered(1))
    out = pl.pallas_call(
        _fnet_dft_kernel,
        grid=(1 + n_macro * per_macro,),
        in_specs=[tile, seq, table, table, _const_spec((2 * gd, gd)),
                  _const_spec((d, d)), _const_spec((1, d)), _const_spec((1, d))],
        out_specs=tile,
        out_shape=jax.ShapeDtypeStruct((bsz, DFT_RADIX, sq, d), F32),
        scratch_shapes=[pltpu.VMEM((2, DFT_RADIX, DFT_TILE, d), BF16),
                        pltpu.VMEM((2, DFT_RADIX, DFT_TILE, d), BF16)],
        compiler_params=_params(1),
        name="fnet_dft",
    )(x4, z, jnp.asarray(tc, BF16), jnp.asarray(ts, BF16),
      jnp.asarray(cdft, BF16), w_out.astype(BF16),
      g.reshape(1, d), b.reshape(1, d))
    return out.reshape(bsz, s, d)


def _fnet_mixer(x, w_in, ln_g, ln_b, w_out, g, b):
    d = x.shape[-1]
    z = _fnet_in(x, w_in, ln_g.reshape(d), ln_b.reshape(d))
    return _fnet_dft(x, z, w_out, g, b)


def _pool_kernel(x_ref, prev_ref, next_ref, w_in_ref, w_grp_ref, scale_ref,
                 w_out_ref, g_ref, b_ref, o_ref, z_scr, quad_scr, *, seq_len):
    ti = pl.program_id(1)
    tm = x_ref.shape[1]
    x = x_ref[0]
    first = ti == 0
    last = ti == pl.num_programs(1) - 1
    x_prev = jnp.where(first, 0.0, prev_ref[0])
    x_next = jnp.where(last, 0.0, next_ref[0])
    x_all = jnp.concatenate([x_prev, x, x_next], axis=0)
    z_scr[...] = _dot(x_all.astype(BF16), w_in_ref[...])

    t = ti * tm + lax.broadcasted_iota(jnp.int32, (tm, 1), 0)
    gd = x.shape[1] // len(POOL_WINDOWS)
    wide = [gi for gi, w in enumerate(POOL_WINDOWS) if w >= 2 * POOL_QUAD]
    wide_col0 = wide[0] * gd
    max_half = max(POOL_WINDOWS) // 2
    quad_row0 = POOL_HALO - max_half
    quad_rows = tm + 2 * max_half
    quad = z_scr[quad_row0:quad_row0 + quad_rows, wide_col0:]
    for k in range(1, POOL_QUAD):
        quad = quad + z_scr[quad_row0 + k:quad_row0 + k + quad_rows, wide_col0:]
    quad_scr[quad_row0:quad_row0 + quad_rows, :] = quad

    mixed_groups = []
    for gi, w in enumerate(POOL_WINDOWS):
        cols = slice(gi * gd, (gi + 1) * gd)
        half = w // 2
        row0 = POOL_HALO - half
        if gi in wide:
            qcols = slice(gi * gd - wide_col0, (gi + 1) * gd - wide_col0)
            win = quad_scr[row0:row0 + tm, qcols]
            for m in range(1, w // POOL_QUAD):
                r = row0 + m * POOL_QUAD
                win = win + quad_scr[r:r + tm, qcols]
        else:
            win = z_scr[row0:row0 + tm, cols]
            for k in range(1, w):
                win = win + z_scr[row0 + k:row0 + k + tm, cols]
        lo = jnp.maximum(t - half, 0)
        hi = jnp.minimum(t - half + w, seq_len)
        inv_count = 1.0 / (hi - lo).astype(F32)
        pooled = win * inv_count - z_scr[POOL_HALO:POOL_HALO + tm, cols]
        mixed_g = _dot(pooled.astype(BF16), w_grp_ref[gi]) * scale_ref[:, cols]
        mixed_groups.append(mixed_g.astype(BF16))
    mixed = jnp.concatenate(mixed_groups, axis=1)
    for rb in range(tm // POOL_ROWS):
        rows = slice(rb * POOL_ROWS, (rb + 1) * POOL_ROWS)
        y = _dot(mixed[rows], w_out_ref[...])
        o_ref[0, rows, :] = _ln(DEEPNORM_ALPHA * x[rows] + y, g_ref[...],
                                b_ref[...])


def _pool_mixer(x, w_in, w_grp, scale, w_out, g, b):
    bsz, s, d = x.shape
    n_g = len(POOL_WINDOWS)
    gd = d // n_g
    tm = POOL_TILE
    n_narrow = sum(w < 2 * POOL_QUAD for w in POOL_WINDOWS)
    per_tile = tm // POOL_HALO
    n_halo = s // POOL_HALO
    tile = pl.BlockSpec((1, tm, d), lambda bi, ti: (bi, ti, 0))
    prev = pl.BlockSpec(
        (1, POOL_HALO, d),
        lambda bi, ti: (bi, jnp.maximum(ti * per_tile - 1, 0), 0))
    nxt = pl.BlockSpec(
        (1, POOL_HALO, d),
        lambda bi, ti: (bi, jnp.minimum((ti + 1) * per_tile, n_halo - 1), 0))
    return pl.pallas_call(
        functools.partial(_pool_kernel, seq_len=s),
        grid=(bsz, s // tm),
        in_specs=[tile, prev, nxt, _const_spec((d, d)),
                  _const_spec((n_g, gd, gd)), _const_spec((1, d)),
                  _const_spec((d, d)), _const_spec((1, d)), _const_spec((1, d))],
        out_specs=tile,
        out_shape=jax.ShapeDtypeStruct((bsz, s, d), F32),
        scratch_shapes=[
            pltpu.VMEM((tm + 2 * POOL_HALO, d), F32),
            pltpu.VMEM((tm + 2 * POOL_HALO, d - n_narrow * gd), F32)],
        compiler_params=_params(2),
        name="pool_mixer",
    )(x, x, x, w_in.astype(BF16), w_grp.astype(BF16), scale.reshape(1, d),
      w_out.astype(BF16), g.reshape(1, d), b.reshape(1, d))


def kernel(x, ln1_g, ln1_b, ffn_w1, ffn_b1, ffn_w2, ffn_b2, ln2_g, ln2_b,
           a_w_in, a_ln_g, a_ln_b, a_w_s, a_b_s, a_w_out,
           b_w_in, b_ln_g, b_ln_b, b_w_out,
           c_w_in, c_w_grp, c_scale, c_w_out):
    bsz, s, d = x.shape
    for i in range(DEPTH):
        kind, j = i % N_MIXERS, i // N_MIXERS
        if kind == 0:
            x = _gmlp_mixer(x.reshape(bsz * s, d), a_w_in[j], a_ln_g[j],
                            a_ln_b[j], a_w_s[j], a_b_s[j], a_w_out[j],
                            ln1_g[i], ln1_b[i])
        elif kind == 1:
            x = _fnet_mixer(x.reshape(bsz, s, d), b_w_in[j], b_ln_g[j],
                            b_ln_b[j], b_w_out[j], ln1_g[i], ln1_b[i])
        else:
            x = _pool_mixer(x.reshape(bsz, s, d), c_w_in[j], c_w_grp[j],
                            c_scale[j], c_w_out[j], ln1_g[i], ln1_b[i])
        x = _ffn(x.reshape(bsz * s, d), ffn_w1[i], ffn_b1[i], ffn_w2[i],
                 ffn_b2[i], ln2_g[i], ln2_b[i])
    return x.reshape(bsz, s, d)
```

```python
import functools
import math

import jax
import jax.numpy as jnp
import numpy as np
from jax import lax
from jax.experimental import pallas as pl
from jax.experimental.pallas import tpu as pltpu

DEPTH = 4
N_MIXERS = 3
DEEPNORM_ALPHA = (2.0 * DEPTH) ** 0.25
LN_EPS = 1e-5

A_CHUNK = 128
A_GROUPS = 8
B_GROUPS = 4
POOL_WINDOWS = (2, 4, 8, 16)
POOL_ROWS = 256
LANES = 128
POOL_HALO = 16
POOL_QUAD = 4

V7X_VMEM_LIMIT_BYTES = 56 * 1024 * 1024
POOL_TILE = 1024
GMLP_TILE = 1024
GMLP_ROWS = 256
GMLP_COLS = 256
FFN_TILE = 1024
FFN_ROW_BLOCKS = (256, 256, 256, 256)
FFN_CHUNK = 1024
DFT_RADIX = 4
DFT_TILE = 256
DFT_QSTEP = 2
FNET_IN_TILE = 256

BF16 = jnp.bfloat16
F32 = jnp.float32


def _ln(r, g, b):
    mu = jnp.mean(r, axis=-1, keepdims=True)
    c = r - mu
    var = jnp.mean(c * c, axis=-1, keepdims=True)
    return c * lax.rsqrt(var + LN_EPS) * g + b


def _gelu_tanh(x):
    c = math.sqrt(2.0 / math.pi)
    half = 0.5 * x
    return half + half * jnp.tanh(x * (c + (c * 0.044715) * (x * x)))


def _dot(a, b):
    return jnp.dot(a, b, preferred_element_type=F32)


def _const_spec(shape):
    zeros = (0,) * len(shape)
    return pl.BlockSpec(shape, lambda *_: zeros, pipeline_mode=pl.Buffered(1))


def _params(n_axes):
    return pltpu.CompilerParams(
        dimension_semantics=("arbitrary",) * n_axes,
        vmem_limit_bytes=V7X_VMEM_LIMIT_BYTES)


def _ffn_kernel(x_ref, w1_ref, b1_ref, w2_ref, b2_ref, g_ref, b_ref, o_ref):
    d_ff = w1_ref.shape[1]
    n_chunks = d_ff // FFN_CHUNK
    x = x_ref[...]
    xb = x.astype(BF16)
    acc = jnp.zeros(x.shape, F32)
    for c in range(n_chunks):
        cols = slice(c * FFN_CHUNK, (c + 1) * FFN_CHUNK)
        h = _dot(xb, w1_ref[:, cols]) + b1_ref[:, cols]
        h = jnp.square(jnp.maximum(h, 0.0)).astype(BF16)
        if c < n_chunks - 1:
            acc = acc + _dot(h, w2_ref[cols, :])
            continue
        row0 = 0
        for n_rows in FFN_ROW_BLOCKS:
            rows = slice(row0, row0 + n_rows)
            row0 += n_rows
            y = acc[rows] + _dot(h[rows], w2_ref[cols, :]) + b2_ref[...]
            o_ref[rows, :] = _ln(DEEPNORM_ALPHA * x[rows] + y, g_ref[...],
                                 b_ref[...])


def _ffn(x2d, w1, b1, w2, b2, g, b):
    n, d = x2d.shape
    f = w1.shape[1]
    row = pl.BlockSpec((FFN_TILE, d), lambda i: (i, 0))
    return pl.pallas_call(
        _ffn_kernel,
        grid=(n // FFN_TILE,),
        in_specs=[row, _const_spec((d, f)), _const_spec((1, f)),
                  _const_spec((f, d)), _const_spec((1, d)),
                  _const_spec((1, d)), _const_spec((1, d))],
        out_specs=row,
        out_shape=jax.ShapeDtypeStruct((n, d), F32),
        compiler_params=_params(1),
        name="ffn",
    )(x2d, w1.astype(BF16), b1.reshape(1, f), w2.astype(BF16),
      b2.reshape(1, d), g.reshape(1, d), b.reshape(1, d))


def _gmlp_kernel(x_ref, w_in_ref, lng_ref, lnb_ref, ws_ref, bs_ref, w_out_ref,
                 g_ref, b_ref, o_ref):
    x = x_ref[...]
    xb = x.astype(BF16)
    width = w_out_ref.shape[0]
    gd = width // A_GROUPS
    n_chunks = x.shape[0] // A_CHUNK

    def proj(col0):
        return jnp.concatenate(
            [_gelu_tanh(_dot(xb, w_in_ref[:, c:c + GMLP_COLS]))
             for c in range(col0, col0 + width, GMLP_COLS)], axis=1)

    v = _ln(proj(width), lng_ref[...], lnb_ref[...]).astype(BF16)
    u = proj(0)
    mixed_groups = []
    for gi in range(A_GROUPS):
        cols = slice(gi * gd, (gi + 1) * gd)
        v_g = jnp.concatenate(
            [v[n * A_CHUNK:(n + 1) * A_CHUNK, cols] for n in range(n_chunks)],
            axis=1)
        mixed_groups.append(_dot(ws_ref[gi], v_g))
    per_block = GMLP_ROWS // A_CHUNK
    for blk in range(n_chunks // per_block):
        rows = slice(blk * GMLP_ROWS, (blk + 1) * GMLP_ROWS)
        mixed = jnp.concatenate(
            [jnp.concatenate(
                [m[:, n * gd:(n + 1) * gd] for m in mixed_groups], axis=1)
             + bs_ref[...]
             for n in range(blk * per_block, (blk + 1) * per_block)], axis=0)
        out = u[rows] * mixed
        y = _dot(out.astype(BF16), w_out_ref[...])
        o_ref[rows, :] = _ln(DEEPNORM_ALPHA * x[rows] + y, g_ref[...], b_ref[...])


def _gmlp_mixer(x2d, w_in, ln_g, ln_b, w_s, b_s, w_out, g, b):
    n, d = x2d.shape
    width = w_out.shape[0]
    gd = width // A_GROUPS
    bias_map = jnp.repeat(b_s.T, gd, axis=1)
    row = pl.BlockSpec((GMLP_TILE, d), lambda i: (i, 0))
    return pl.pallas_call(
        _gmlp_kernel,
        grid=(n // GMLP_TILE,),
        in_specs=[row, _const_spec((d, 2 * width)), _const_spec((1, width)),
                  _const_spec((1, width)),
                  _const_spec((A_GROUPS, A_CHUNK, A_CHUNK)),
                  _const_spec((A_CHUNK, width)), _const_spec((width, d)),
                  _const_spec((1, d)), _const_spec((1, d))],
        out_specs=row,
        out_shape=jax.ShapeDtypeStruct((n, d), F32),
        compiler_params=_params(1),
        name="gmlp_mixer",
    )(x2d, w_in.astype(BF16), ln_g.reshape(1, width), ln_b.reshape(1, width),
      w_s.astype(BF16), bias_map, w_out.astype(BF16),
      g.reshape(1, d), b.reshape(1, d))


def _fnet_in_kernel(x_ref, w_in_ref, lng_ref, lnb_ref, z_ref, x_scr):
    d = w_in_ref.shape[0]
    rows = x_ref.shape[0] // DFT_RADIX
    n_lane_blocks = d // LANES
    for c in range(n_lane_blocks):
        x_scr[c] = x_ref[:, c * LANES:(c + 1) * LANES]
    xs = jnp.concatenate(
        [jnp.concatenate(
            [x_scr[pl.ds(c, 1), pl.ds(r, rows, stride=DFT_RADIX), :][0]
             for c in range(n_lane_blocks)], axis=1)
         for r in range(DFT_RADIX)], axis=0)
    z = _dot(xs.astype(BF16), w_in_ref[...])
    gd = d // B_GROUPS
    for gi in range(B_GROUPS):
        cols = slice(gi * gd, (gi + 1) * gd)
        zn = _ln(z[:, cols], lng_ref[:, cols], lnb_ref[:, cols]).astype(BF16)
        for r in range(DFT_RADIX):
            z_ref[0, r, :, cols] = zn[r * rows:(r + 1) * rows]


def _fnet_in(x, w_in, ln_g, ln_b):
    bsz, s, d = x.shape
    sq = s // DFT_RADIX
    return pl.pallas_call(
        _fnet_in_kernel,
        grid=(bsz, sq // FNET_IN_TILE),
        in_specs=[pl.BlockSpec((None, DFT_RADIX * FNET_IN_TILE, d),
                               lambda bi, ti: (bi, ti, 0)),
                  _const_spec((d, d)), _const_spec((1, d)), _const_spec((1, d))],
        out_specs=pl.BlockSpec((1, DFT_RADIX, FNET_IN_TILE, d),
                               lambda bi, ti: (bi, 0, ti, 0)),
        out_shape=jax.ShapeDtypeStruct((bsz, DFT_RADIX, sq, d), BF16),
        scratch_shapes=[pltpu.VMEM(
            (d // LANES, DFT_RADIX * FNET_IN_TILE, LANES), F32)],
        compiler_params=_params(2),
        name="fnet_in",
    )(x, w_in.astype(BF16), ln_g.reshape(1, d), ln_b.reshape(1, d))


def _fnet_dft_kernel(x_ref, z_ref, tc_ref, ts_ref, cdft_ref, w_out_ref,
                     g_ref, b_ref, o_ref, p_scr, q_scr):
    step = pl.program_id(0)
    per_macro = DFT_RADIX // DFT_QSTEP
    n_macro = (pl.num_programs(0) - 1) // per_macro
    mi = jnp.maximum(step - 1, 0) // per_macro
    qi = jnp.maximum(step - 1, 0) % per_macro
    rows = DFT_QSTEP * DFT_TILE

    @pl.when(step > 0)
    def _():
        slot = mi % 2
        quads = pl.ds(qi * DFT_QSTEP, DFT_QSTEP)
        x = x_ref[0].reshape(rows, x_ref.shape[-1])
        p = p_scr[slot, quads].reshape(rows, p_scr.shape[-1])
        q = q_scr[slot, quads].reshape(rows, q_scr.shape[-1])
        gd = p.shape[1] // B_GROUPS
        f_groups = []
        for gi in range(B_GROUPS):
            cols = slice(gi * gd, (gi + 1) * gd)
            pq = jnp.concatenate([p[:, cols], q[:, cols]], axis=1)
            f_groups.append(_dot(pq, cdft_ref[...]))
        f = jnp.concatenate(f_groups, axis=1).astype(BF16)
        for j in range(DFT_QSTEP):
            blk = slice(j * DFT_TILE, (j + 1) * DFT_TILE)
            y = _dot(f[blk], w_out_ref[...])
            o_ref[0, j] = _ln(DEEPNORM_ALPHA * x[blk] + y, g_ref[...], b_ref[...])

    starts_macro = jnp.where(step == 0, 0, mi + 1)

    @pl.when((step == 0) | ((qi == per_macro - 1) & (mi + 1 < n_macro)))
    def _():
        slot = starts_macro % 2
        tc = {r: _dot(tc_ref[r], z_ref[0, r]) for r in (0, 2, 1, 3)}
        ts = {r: _dot(ts_ref[r], z_ref[0, r]) for r in (0, 2, 1, 3)}
        c_ev_p, c_ev_m = tc[0] + tc[2], tc[0] - tc[2]
        c_od_p, c_od_m = tc[1] + tc[3], tc[1] - tc[3]
        s_ev_p, s_ev_m = ts[0] + ts[2], ts[0] - ts[2]
        s_od_p, s_od_m = ts[1] + ts[3], ts[1] - ts[3]
        p_scr[slot, 0] = (c_ev_p + c_od_p).astype(BF16)
        q_scr[slot, 0] = (s_ev_p + s_od_p).astype(BF16)
        p_scr[slot, 1] = (c_ev_m - s_od_m).astype(BF16)
        q_scr[slot, 1] = (s_ev_m + c_od_m).astype(BF16)
        p_scr[slot, 2] = (c_ev_p - c_od_p).astype(BF16)
        q_scr[slot, 2] = (s_ev_p - s_od_p).astype(BF16)
        p_scr[slot, 3] = (c_ev_m + s_od_m).astype(BF16)
        q_scr[slot, 3] = (s_ev_m - c_od_m).astype(BF16)


def _cos_sin(index, period):
    ang = (index % period).astype(np.float64) * (2.0 * math.pi / period)
    return np.cos(ang), np.sin(ang)


def _position_dft_tables(s):
    sq = s // DFT_RADIX
    k = np.arange(sq, dtype=np.int64)[None, :, None]
    sp = np.arange(sq, dtype=np.int64)[None, None, :]
    r = np.arange(DFT_RADIX, dtype=np.int64)[:, None, None]
    return _cos_sin(k * (DFT_RADIX * sp + r), s)


def _channel_dft_table(gd, norm):
    j = np.arange(gd, dtype=np.int64)
    cos_c, sin_c = _cos_sin(j[:, None] * j[None, :], gd)
    return np.concatenate([cos_c, -sin_c], axis=0) * norm


def _fnet_dft(x, z, w_out, g, b):
    bsz, s, d = x.shape
    gd = d // B_GROUPS
    sq = s // DFT_RADIX
    tc, ts = _position_dft_tables(s)
    cdft = _channel_dft_table(gd, 1.0 / math.sqrt(s * gd))
    x4 = x.reshape(bsz, DFT_RADIX, sq, d)
    n_kt = sq // DFT_TILE
    n_macro = bsz * n_kt

    per_macro = DFT_RADIX // DFT_QSTEP

    def finishing(step):
        return jnp.maximum(step - 1, 0) // per_macro

    def transforming(step):
        return jnp.where(step == 0, 0,
                         jnp.minimum(finishing(step) + 1, n_macro - 1))

    tile = pl.BlockSpec(
        (1, DFT_QSTEP, DFT_TILE, d),
        lambda step: (finishing(step) // n_kt,
                      jnp.maximum(step - 1, 0) % per_macro,
                      finishing(step) % n_kt, 0))
    seq = pl.BlockSpec((1, DFT_RADIX, sq, d),
                       lambda step: (transforming(step) // n_kt, 0, 0, 0))
    table = pl.BlockSpec((DFT_RADIX, DFT_TILE, sq),
                         lambda step: (0, transforming(step) % n_kt, 0))
    out = pl.pallas_call(
        _fnet_dft_kernel,
        grid=(1 + n_macro * per_macro,),
        in_specs=[tile, seq, table, table, _const_spec((2 * gd, gd)),
                  _const_spec((d, d)), _const_spec((1, d)), _const_spec((1, d))],
        out_specs=tile,
        out_shape=jax.ShapeDtypeStruct((bsz, DFT_RADIX, sq, d), F32),
        scratch_shapes=[pltpu.VMEM((2, DFT_RADIX, DFT_TILE, d), BF16),
                        pltpu.VMEM((2, DFT_RADIX, DFT_TILE, d), BF16)],
        compiler_params=_params(1),
        name="fnet_dft",
    )(x4, z, jnp.asarray(tc, BF16), jnp.asarray(ts, BF16),
      jnp.asarray(cdft, BF16), w_out.astype(BF16),
      g.reshape(1, d), b.reshape(1, d))
    return out.reshape(bsz, s, d)


def _fnet_mixer(x, w_in, ln_g, ln_b, w_out, g, b):
    d = x.shape[-1]
    z = _fnet_in(x, w_in, ln_g.reshape(d), ln_b.reshape(d))
    return _fnet_dft(x, z, w_out, g, b)


def _pool_kernel(x_ref, prev_ref, next_ref, w_in_ref, w_grp_ref, scale_ref,
                 w_out_ref, g_ref, b_ref, o_ref, z_scr, quad_scr, *, seq_len):
    ti = pl.program_id(1)
    tm = x_ref.shape[1]
    x = x_ref[0]
    first = ti == 0
    last = ti == pl.num_programs(1) - 1
    x_prev = jnp.where(first, 0.0, prev_ref[0])
    x_next = jnp.where(last, 0.0, next_ref[0])
    x_all = jnp.concatenate([x_prev, x, x_next], axis=0)
    z_scr[...] = _dot(x_all.astype(BF16), w_in_ref[...])

    t = ti * tm + lax.broadcasted_iota(jnp.int32, (tm, 1), 0)
    gd = x.shape[1] // len(POOL_WINDOWS)
    wide = [gi for gi, w in enumerate(POOL_WINDOWS) if w >= 2 * POOL_QUAD]
    wide_col0 = wide[0] * gd
    max_half = max(POOL_WINDOWS) // 2
    quad_row0 = POOL_HALO - max_half
    quad_rows = tm + 2 * max_half
    quad = z_scr[quad_row0:quad_row0 + quad_rows, wide_col0:]
    for k in range(1, POOL_QUAD):
        quad = quad + z_scr[quad_row0 + k:quad_row0 + k + quad_rows, wide_col0:]
    quad_scr[quad_row0:quad_row0 + quad_rows, :] = quad

    mixed_groups = []
    for gi, w in enumerate(POOL_WINDOWS):
        cols = slice(gi * gd, (gi + 1) * gd)
        half = w // 2
        row0 = POOL_HALO - half
        if gi in wide:
            qcols = slice(gi * gd - wide_col0, (gi + 1) * gd - wide_col0)
            win = quad_scr[row0:row0 + tm, qcols]
            for m in range(1, w // POOL_QUAD):
                r = row0 + m * POOL_QUAD
                win = win + quad_scr[r:r + tm, qcols]
        else:
            win = z_scr[row0:row0 + tm, cols]
            for k in range(1, w):
                win = win + z_scr[row0 + k:row0 + k + tm, cols]
        lo = jnp.maximum(t - half, 0)
        hi = jnp.minimum(t - half + w, seq_len)
        inv_count = 1.0 / (hi - lo).astype(F32)
        pooled = win * inv_count - z_scr[POOL_HALO:POOL_HALO + tm, cols]
        mixed_g = _dot(pooled.astype(BF16), w_grp_ref[gi]) * scale_ref[:, cols]
        mixed_groups.append(mixed_g.astype(BF16))
    mixed = jnp.concatenate(mixed_groups, axis=1)
    for rb in range(tm // POOL_ROWS):
        rows = slice(rb * POOL_ROWS, (rb + 1) * POOL_ROWS)
        y = _dot(mixed[rows], w_out_ref[...])
        o_ref[0, rows, :] = _ln(DEEPNORM_ALPHA * x[rows] + y, g_ref[...],
                                b_ref[...])


def _pool_mixer(x, w_in, w_grp, scale, w_out, g, b):
    bsz, s, d = x.shape
    n_g = len(POOL_WINDOWS)
    gd = d // n_g
    tm = POOL_TILE
    n_narrow = sum(w < 2 * POOL_QUAD for w in POOL_WINDOWS)
    per_tile = tm // POOL_HALO
    n_halo = s // POOL_HALO
    tile = pl.BlockSpec((1, tm, d), lambda bi, ti: (bi, ti, 0))
    prev = pl.BlockSpec(
        (1, POOL_HALO, d),
        lambda bi, ti: (bi, jnp.maximum(ti * per_tile - 1, 0), 0))
    nxt = pl.BlockSpec(
        (1, POOL_HALO, d),
        lambda bi, ti: (bi, jnp.minimum((ti + 1) * per_tile, n_halo - 1), 0))
    return pl.pallas_call(
        functools.partial(_pool_kernel, seq_len=s),
        grid=(bsz, s // tm),
        in_specs=[tile, prev, nxt, _const_spec((d, d)),
                  _const_spec((n_g, gd, gd)), _const_spec((1, d)),
                  _const_spec((d, d)), _const_spec((1, d)), _const_spec((1, d))],
        out_specs=tile,
        out_shape=jax.ShapeDtypeStruct((bsz, s, d), F32),
        scratch_shapes=[
            pltpu.VMEM((tm + 2 * POOL_HALO, d), F32),
            pltpu.VMEM((tm + 2 * POOL_HALO, d - n_narrow * gd), F32)],
        compiler_params=_params(2),
        name="pool_mixer",
    )(x, x, x, w_in.astype(BF16), w_grp.astype(BF16), scale.reshape(1, d),
      w_out.astype(BF16), g.reshape(1, d), b.reshape(1, d))


def kernel(x, ln1_g, ln1_b, ffn_w1, ffn_b1, ffn_w2, ffn_b2, ln2_g, ln2_b,
           a_w_in, a_ln_g, a_ln_b, a_w_s, a_b_s, a_w_out,
           b_w_in, b_ln_g, b_ln_b, b_w_out,
           c_w_in, c_w_grp, c_scale, c_w_out):
    bsz, s, d = x.shape
    for i in range(DEPTH):
        kind, j = i % N_MIXERS, i // N_MIXERS
        if kind == 0:
            x = _gmlp_mixer(x.reshape(bsz * s, d), a_w_in[j], a_ln_g[j],
                            a_ln_b[j], a_w_s[j], a_b_s[j], a_w_out[j],
                            ln1_g[i], ln1_b[i])
        elif kind == 1:
            x = _fnet_mixer(x.reshape(bsz, s, d), b_w_in[j], b_ln_g[j],
                            b_ln_b[j], b_w_out[j], ln1_g[i], ln1_b[i])
        else:
            x = _pool_mixer(x.reshape(bsz, s, d), c_w_in[j], c_w_grp[j],
                            c_scale[j], c_w_out[j], ln1_g[i], ln1_b[i])
        x = _ffn(x.reshape(bsz * s, d), ffn_w1[i], ffn_b1[i], ffn_w2[i],
                 ffn_b2[i], ln2_g[i], ln2_b[i])
    return x.reshape(bsz, s, d)
```

```python
import functools
import math

import jax
import jax.numpy as jnp
import numpy as np
from jax import lax
from jax.experimental import pallas as pl
from jax.experimental.pallas import tpu as pltpu

DEPTH = 4
N_MIXERS = 3
DEEPNORM_ALPHA = (2.0 * DEPTH) ** 0.25
LN_EPS = 1e-5

A_CHUNK = 128
A_GROUPS = 8
B_GROUPS = 4
POOL_WINDOWS = (2, 4, 8, 16)
POOL_ROWS = 256
LANES = 128
POOL_HALO = 16
POOL_QUAD = 4

V7X_VMEM_LIMIT_BYTES = 56 * 1024 * 1024
POOL_TILE = 1024
GMLP_TILE = 1024
GMLP_ROWS = 256
GMLP_COLS = 256
FFN_TILE = 1024
FFN_ROW_BLOCKS = (256, 256, 256, 256)
FFN_CHUNK = 1024
DFT_RADIX = 4
DFT_TILE = 256
DFT_QSTEP = 2
FNET_IN_TILE = 256

BF16 = jnp.bfloat16
F32 = jnp.float32


def _ln(r, g, b):
    mu = jnp.mean(r, axis=-1, keepdims=True)
    c = r - mu
    var = jnp.mean(c * c, axis=-1, keepdims=True)
    return c * lax.rsqrt(var + LN_EPS) * g + b


def _gelu_tanh(x):
    c = math.sqrt(2.0 / math.pi)
    half = 0.5 * x
    return half + half * jnp.tanh(x * (c + (c * 0.044715) * (x * x)))


def _dot(a, b):
    return jnp.dot(a, b, preferred_element_type=F32)


def _const_spec(shape):
    zeros = (0,) * len(shape)
    return pl.BlockSpec(shape, lambda *_: zeros, pipeline_mode=pl.Buffered(1))


def _layer_spec(shape, layer):
    zeros = (0,) * len(shape)
    return pl.BlockSpec((None,) + tuple(shape), lambda *_: (layer,) + zeros,
                        pipeline_mode=pl.Buffered(1))


def _params(n_axes):
    return pltpu.CompilerParams(
        dimension_semantics=("arbitrary",) * n_axes,
        vmem_limit_bytes=V7X_VMEM_LIMIT_BYTES)


def _ffn_kernel(x_ref, w1_ref, b1_ref, w2_ref, b2_ref, g_ref, b_ref, o_ref):
    d_ff = w1_ref.shape[1]
    n_chunks = d_ff // FFN_CHUNK
    x = x_ref[...]
    xb = x.astype(BF16)
    acc = jnp.zeros(x.shape, F32)
    for c in range(n_chunks):
        cols = slice(c * FFN_CHUNK, (c + 1) * FFN_CHUNK)
        h = _dot(xb, w1_ref[:, cols]) + b1_ref[:, cols]
        h = jnp.square(jnp.maximum(h, 0.0)).astype(BF16)
        if c < n_chunks - 1:
            acc = acc + _dot(h, w2_ref[cols, :])
            continue
        row0 = 0
        for n_rows in FFN_ROW_BLOCKS:
            rows = slice(row0, row0 + n_rows)
            row0 += n_rows
            y = acc[rows] + _dot(h[rows], w2_ref[cols, :]) + b2_ref[...]
            o_ref[rows, :] = _ln(DEEPNORM_ALPHA * x[rows] + y, g_ref[...],
                                 b_ref[...])


def _ffn(x2d, w1_all, b1, w2_all, b2, g, b, layer):
    n, d = x2d.shape
    f = w1_all.shape[2]
    row = pl.BlockSpec((FFN_TILE, d), lambda i: (i, 0))
    return pl.pallas_call(
        _ffn_kernel,
        grid=(n // FFN_TILE,),
        in_specs=[row, _layer_spec((d, f), layer), _const_spec((1, f)),
                  _layer_spec((f, d), layer), _const_spec((1, d)),
                  _const_spec((1, d)), _const_spec((1, d))],
        out_specs=row,
        out_shape=jax.ShapeDtypeStruct((n, d), F32),
        compiler_params=_params(1),
        name="ffn",
    )(x2d, w1_all, b1.reshape(1, f), w2_all,
      b2.reshape(1, d), g.reshape(1, d), b.reshape(1, d))


def _gmlp_kernel(x_ref, w_in_ref, lng_ref, lnb_ref, ws_ref, bs_ref, w_out_ref,
                 g_ref, b_ref, o_ref):
    x = x_ref[...]
    xb = x.astype(BF16)
    width = w_out_ref.shape[0]
    gd = width // A_GROUPS
    n_chunks = x.shape[0] // A_CHUNK

    def proj(col0):
        return jnp.concatenate(
            [_gelu_tanh(_dot(xb, w_in_ref[:, c:c + GMLP_COLS]))
             for c in range(col0, col0 + width, GMLP_COLS)], axis=1)

    v = _ln(proj(width), lng_ref[...], lnb_ref[...]).astype(BF16)
    u = proj(0)
    mixed_groups = []
    for gi in range(A_GROUPS):
        cols = slice(gi * gd, (gi + 1) * gd)
        v_g = jnp.concatenate(
            [v[n * A_CHUNK:(n + 1) * A_CHUNK, cols] for n in range(n_chunks)],
            axis=1)
        mixed_groups.append(_dot(ws_ref[gi], v_g))
    per_block = GMLP_ROWS // A_CHUNK
    for blk in range(n_chunks // per_block):
        rows = slice(blk * GMLP_ROWS, (blk + 1) * GMLP_ROWS)
        mixed = jnp.concatenate(
            [jnp.concatenate(
                [m[:, n * gd:(n + 1) * gd] for m in mixed_groups], axis=1)
             + bs_ref[...]
             for n in range(blk * per_block, (blk + 1) * per_block)], axis=0)
        out = u[rows] * mixed
        y = _dot(out.astype(BF16), w_out_ref[...])
        o_ref[rows, :] = _ln(DEEPNORM_ALPHA * x[rows] + y, g_ref[...], b_ref[...])


def _gmlp_mixer(x2d, w_in_all, ln_g, ln_b, w_s_all, b_s, w_out_all, g, b, layer):
    n, d = x2d.shape
    width = w_out_all.shape[1]
    gd = width // A_GROUPS
    bias_map = jnp.repeat(b_s.T, gd, axis=1)
    row = pl.BlockSpec((GMLP_TILE, d), lambda i: (i, 0))
    return pl.pallas_call(
        _gmlp_kernel,
        grid=(n // GMLP_TILE,),
        in_specs=[row, _layer_spec((d, 2 * width), layer),
                  _const_spec((1, width)), _const_spec((1, width)),
                  _layer_spec((A_GROUPS, A_CHUNK, A_CHUNK), layer),
                  _const_spec((A_CHUNK, width)),
                  _layer_spec((width, d), layer),
                  _const_spec((1, d)), _const_spec((1, d))],
        out_specs=row,
        out_shape=jax.ShapeDtypeStruct((n, d), F32),
        compiler_params=_params(1),
        name="gmlp_mixer",
    )(x2d, w_in_all, ln_g.reshape(1, width), ln_b.reshape(1, width),
      w_s_all, bias_map, w_out_all, g.reshape(1, d), b.reshape(1, d))


def _fnet_in_kernel(x_ref, w_in_ref, lng_ref, lnb_ref, z_ref, x_scr):
    d = w_in_ref.shape[0]
    rows = x_ref.shape[0] // DFT_RADIX
    n_lane_blocks = d // LANES
    for c in range(n_lane_blocks):
        x_scr[c] = x_ref[:, c * LANES:(c + 1) * LANES]
    xs = jnp.concatenate(
        [jnp.concatenate(
            [x_scr[pl.ds(c, 1), pl.ds(r, rows, stride=DFT_RADIX), :][0]
             for c in range(n_lane_blocks)], axis=1)
         for r in range(DFT_RADIX)], axis=0)
    z = _dot(xs.astype(BF16), w_in_ref[...])
    gd = d // B_GROUPS
    for gi in range(B_GROUPS):
        cols = slice(gi * gd, (gi + 1) * gd)
        zn = _ln(z[:, cols], lng_ref[:, cols], lnb_ref[:, cols]).astype(BF16)
        for r in range(DFT_RADIX):
            z_ref[0, r, :, cols] = zn[r * rows:(r + 1) * rows]


def _fnet_in(x, w_in, ln_g, ln_b):
    bsz, s, d = x.shape
    sq = s // DFT_RADIX
    return pl.pallas_call(
        _fnet_in_kernel,
        grid=(bsz, sq // FNET_IN_TILE),
        in_specs=[pl.BlockSpec((None, DFT_RADIX * FNET_IN_TILE, d),
                               lambda bi, ti: (bi, ti, 0)),
                  _const_spec((d, d)), _const_spec((1, d)), _const_spec((1, d))],
        out_specs=pl.BlockSpec((1, DFT_RADIX, FNET_IN_TILE, d),
                               lambda bi, ti: (bi, 0, ti, 0)),
        out_shape=jax.ShapeDtypeStruct((bsz, DFT_RADIX, sq, d), BF16),
        scratch_shapes=[pltpu.VMEM(
            (d // LANES, DFT_RADIX * FNET_IN_TILE, LANES), F32)],
        compiler_params=_params(2),
        name="fnet_in",
    )(x, w_in.astype(BF16), ln_g.reshape(1, d), ln_b.reshape(1, d))


def _fnet_dft_kernel(x_ref, z_ref, tc_ref, ts_ref, cdft_ref, w_out_ref,
                     g_ref, b_ref, o_ref, p_scr, q_scr):
    step = pl.program_id(0)
    per_macro = DFT_RADIX // DFT_QSTEP
    n_macro = (pl.num_programs(0) - 1) // per_macro
    mi = jnp.maximum(step - 1, 0) // per_macro
    qi = jnp.maximum(step - 1, 0) % per_macro
    rows = DFT_QSTEP * DFT_TILE

    @pl.when(step > 0)
    def _():
        slot = mi % 2
        quads = pl.ds(qi * DFT_QSTEP, DFT_QSTEP)
        x = x_ref[0].reshape(rows, x_ref.shape[-1])
        p = p_scr[slot, quads].reshape(rows, p_scr.shape[-1])
        q = q_scr[slot, quads].reshape(rows, q_scr.shape[-1])
        gd = p.shape[1] // B_GROUPS
        f_groups = []
        for gi in range(B_GROUPS):
            cols = slice(gi * gd, (gi + 1) * gd)
            pq = jnp.concatenate([p[:, cols], q[:, cols]], axis=1)
            f_groups.append(_dot(pq, cdft_ref[...]))
        f = jnp.concatenate(f_groups, axis=1).astype(BF16)
        for j in range(DFT_QSTEP):
            blk = slice(j * DFT_TILE, (j + 1) * DFT_TILE)
            y = _dot(f[blk], w_out_ref[...])
            o_ref[0, j] = _ln(DEEPNORM_ALPHA * x[blk] + y, g_ref[...], b_ref[...])

    starts_macro = jnp.where(step == 0, 0, mi + 1)

    @pl.when((step == 0) | ((qi == per_macro - 1) & (mi + 1 < n_macro)))
    def _():
        slot = starts_macro % 2
        tc = {r: _dot(tc_ref[r], z_ref[0, r]) for r in (0, 2, 1, 3)}
        ts = {r: _dot(ts_ref[r], z_ref[0, r]) for r in (0, 2, 1, 3)}
        c_ev_p, c_ev_m = tc[0] + tc[2], tc[0] - tc[2]
        c_od_p, c_od_m = tc[1] + tc[3], tc[1] - tc[3]
        s_ev_p, s_ev_m = ts[0] + ts[2], ts[0] - ts[2]
        s_od_p, s_od_m = ts[1] + ts[3], ts[1] - ts[3]
        p_scr[slot, 0] = (c_ev_p + c_od_p).astype(BF16)
        q_scr[slot, 0] = (s_ev_p + s_od_p).astype(BF16)
        p_scr[slot, 1] = (c_ev_m - s_od_m).astype(BF16)
        q_scr[slot, 1] = (s_ev_m + c_od_m).astype(BF16)
        p_scr[slot, 2] = (c_ev_p - c_od_p).astype(BF16)
        q_scr[slot, 2] = (s_ev_p - s_od_p).astype(BF16)
        p_scr[slot, 3] = (c_ev_m + s_od_m).astype(BF16)
        q_scr[slot, 3] = (s_ev_m - c_od_m).astype(BF16)


def _cos_sin(index, period):
    ang = (index % period).astype(np.float64) * (2.0 * math.pi / period)
    return np.cos(ang), np.sin(ang)


def _position_dft_tables(s):
    sq = s // DFT_RADIX
    k = np.arange(sq, dtype=np.int64)[None, :, None]
    sp = np.arange(sq, dtype=np.int64)[None, None, :]
    r = np.arange(DFT_RADIX, dtype=np.int64)[:, None, None]
    return _cos_sin(k * (DFT_RADIX * sp + r), s)


def _channel_dft_table(gd, norm):
    j = np.arange(gd, dtype=np.int64)
    cos_c, sin_c = _cos_sin(j[:, None] * j[None, :], gd)
    return np.concatenate([cos_c, -sin_c], axis=0) * norm


def _fnet_dft(x, z, w_out, g, b):
    bsz, s, d = x.shape
    gd = d // B_GROUPS
    sq = s // DFT_RADIX
    tc, ts = _position_dft_tables(s)
    cdft = _channel_dft_table(gd, 1.0 / math.sqrt(s * gd))
    x4 = x.reshape(bsz, DFT_RADIX, sq, d)
    n_kt = sq // DFT_TILE
    n_macro = bsz * n_kt

    per_macro = DFT_RADIX // DFT_QSTEP

    def finishing(step):
        return jnp.maximum(step - 1, 0) // per_macro

    def transforming(step):
        return jnp.where(step == 0, 0,
                         jnp.minimum(finishing(step) + 1, n_macro - 1))

    tile = pl.BlockSpec(
        (1, DFT_QSTEP, DFT_TILE, d),
        lambda step: (finishing(step) // n_kt,
                      jnp.maximum(step - 1, 0) % per_macro,
                      finishing(step) % n_kt, 0))
    seq = pl.BlockSpec((1, DFT_RADIX, sq, d),
                       lambda step: (transforming(step) // n_kt, 0, 0, 0))
    table = pl.BlockSpec((DFT_RADIX, DFT_TILE, sq),
                         lambda step: (0, transforming(step) % n_kt, 0))
    out = pl.pallas_call(
        _fnet_dft_kernel,
        grid=(1 + n_macro * per_macro,),
        in_specs=[tile, seq, table, table, _const_spec((2 * gd, gd)),
                  _const_spec((d, d)), _const_spec((1, d)), _const_spec((1, d))],
        out_specs=tile,
        out_shape=jax.ShapeDtypeStruct((bsz, DFT_RADIX, sq, d), F32),
        scratch_shapes=[pltpu.VMEM((2, DFT_RADIX, DFT_TILE, d), BF16),
                        pltpu.VMEM((2, DFT_RADIX, DFT_TILE, d), BF16)],
        compiler_params=_params(1),
        name="fnet_dft",
    )(x4, z, jnp.asarray(tc, BF16), jnp.asarray(ts, BF16),
      jnp.asarray(cdft, BF16), w_out.astype(BF16),
      g.reshape(1, d), b.reshape(1, d))
    return out.reshape(bsz, s, d)


def _fnet_mixer(x, w_in, ln_g, ln_b, w_out, g, b):
    d = x.shape[-1]
    z = _fnet_in(x, w_in, ln_g.reshape(d), ln_b.reshape(d))
    return _fnet_dft(x, z, w_out, g, b)


def _pool_kernel(x_ref, prev_ref, next_ref, w_in_ref, w_grp_ref, scale_ref,
                 w_out_ref, g_ref, b_ref, o_ref, z_scr, quad_scr, *, seq_len):
    ti = pl.program_id(1)
    tm = x_ref.shape[1]
    x = x_ref[0]
    first = ti == 0
    last = ti == pl.num_programs(1) - 1
    x_prev = jnp.where(first, 0.0, prev_ref[0])
    x_next = jnp.where(last, 0.0, next_ref[0])
    x_all = jnp.concatenate([x_prev, x, x_next], axis=0)
    z_scr[...] = _dot(x_all.astype(BF16), w_in_ref[...])

    t = ti * tm + lax.broadcasted_iota(jnp.int32, (tm, 1), 0)
    gd = x.shape[1] // len(POOL_WINDOWS)
    wide = [gi for gi, w in enumerate(POOL_WINDOWS) if w >= 2 * POOL_QUAD]
    wide_col0 = wide[0] * gd
    max_half = max(POOL_WINDOWS) // 2
    quad_row0 = POOL_HALO - max_half
    quad_rows = tm + 2 * max_half
    quad = z_scr[quad_row0:quad_row0 + quad_rows, wide_col0:]
    for k in range(1, POOL_QUAD):
        quad = quad + z_scr[quad_row0 + k:quad_row0 + k + quad_rows, wide_col0:]
    quad_scr[quad_row0:quad_row0 + quad_rows, :] = quad

    mixed_groups = []
    for gi, w in enumerate(POOL_WINDOWS):
        cols = slice(gi * gd, (gi + 1) * gd)
        half = w // 2
        row0 = POOL_HALO - half
        if gi in wide:
            qcols = slice(gi * gd - wide_col0, (gi + 1) * gd - wide_col0)
            win = quad_scr[row0:row0 + tm, qcols]
            for m in range(1, w // POOL_QUAD):
                r = row0 + m * POOL_QUAD
                win = win + quad_scr[r:r + tm, qcols]
        else:
            win = z_scr[row0:row0 + tm, cols]
            for k in range(1, w):
                win = win + z_scr[row0 + k:row0 + k + tm, cols]
        lo = jnp.maximum(t - half, 0)
        hi = jnp.minimum(t - half + w, seq_len)
        inv_count = 1.0 / (hi - lo).astype(F32)
        pooled = win * inv_count - z_scr[POOL_HALO:POOL_HALO + tm, cols]
        mixed_g = _dot(pooled.astype(BF16), w_grp_ref[gi]) * scale_ref[:, cols]
        mixed_groups.append(mixed_g.astype(BF16))
    mixed = jnp.concatenate(mixed_groups, axis=1)
    for rb in range(tm // POOL_ROWS):
        rows = slice(rb * POOL_ROWS, (rb + 1) * POOL_ROWS)
        y = _dot(mixed[rows], w_out_ref[...])
        o_ref[0, rows, :] = _ln(DEEPNORM_ALPHA * x[rows] + y, g_ref[...],
                                b_ref[...])


def _pool_mixer(x, w_in, w_grp, scale, w_out, g, b):
    bsz, s, d = x.shape
    n_g = len(POOL_WINDOWS)
    gd = d // n_g
    tm = POOL_TILE
    n_narrow = sum(w < 2 * POOL_QUAD for w in POOL_WINDOWS)
    per_tile = tm // POOL_HALO
    n_halo = s // POOL_HALO
    tile = pl.BlockSpec((1, tm, d), lambda bi, ti: (bi, ti, 0))
    prev = pl.BlockSpec(
        (1, POOL_HALO, d),
        lambda bi, ti: (bi, jnp.maximum(ti * per_tile - 1, 0), 0))
    nxt = pl.BlockSpec(
        (1, POOL_HALO, d),
        lambda bi, ti: (bi, jnp.minimum((ti + 1) * per_tile, n_halo - 1), 0))
    return pl.pallas_call(
        functools.partial(_pool_kernel, seq_len=s),
        grid=(bsz, s // tm),
        in_specs=[tile, prev, nxt, _const_spec((d, d)),
                  _const_spec((n_g, gd, gd)), _const_spec((1, d)),
                  _const_spec((d, d)), _const_spec((1, d)), _const_spec((1, d))],
        out_specs=tile,
        out_shape=jax.ShapeDtypeStruct((bsz, s, d), F32),
        scratch_shapes=[
            pltpu.VMEM((tm + 2 * POOL_HALO, d), F32),
            pltpu.VMEM((tm + 2 * POOL_HALO, d - n_narrow * gd), F32)],
        compiler_params=_params(2),
        name="pool_mixer",
    )(x, x, x, w_in.astype(BF16), w_grp.astype(BF16), scale.reshape(1, d),
      w_out.astype(BF16), g.reshape(1, d), b.reshape(1, d))


def kernel(x, ln1_g, ln1_b, ffn_w1, ffn_b1, ffn_w2, ffn_b2, ln2_g, ln2_b,
           a_w_in, a_ln_g, a_ln_b, a_w_s, a_b_s, a_w_out,
           b_w_in, b_ln_g, b_ln_b, b_w_out,
           c_w_in, c_w_grp, c_scale, c_w_out):
    bsz, s, d = x.shape
    ffn_w1_b, ffn_w2_b = ffn_w1.astype(BF16), ffn_w2.astype(BF16)
    a_w_in_b, a_w_s_b = a_w_in.astype(BF16), a_w_s.astype(BF16)
    a_w_out_b = a_w_out.astype(BF16)
    for i in range(DEPTH):
        kind, j = i % N_MIXERS, i // N_MIXERS
        if kind == 0:
            x = _gmlp_mixer(x.reshape(bsz * s, d), a_w_in_b, a_ln_g[j],
                            a_ln_b[j], a_w_s_b, a_b_s[j], a_w_out_b,
                            ln1_g[i], ln1_b[i], j)
        elif kind == 1:
            x = _fnet_mixer(x.reshape(bsz, s, d), b_w_in[j], b_ln_g[j],
                            b_ln_b[j], b_w_out[j], ln1_g[i], ln1_b[i])
        else:
            x = _pool_mixer(x.reshape(bsz, s, d), c_w_in[j], c_w_grp[j],
                            c_scale[j], c_w_out[j], ln1_g[i], ln1_b[i])
        x = _ffn(x.reshape(bsz * s, d), ffn_w1_b, ffn_b1[i], ffn_w2_b,
                 ffn_b2[i], ln2_g[i], ln2_b[i], i)
    return x.reshape(bsz, s, d)
```

```python
import functools
import math

import jax
import jax.numpy as jnp
import numpy as np
from jax import lax
from jax.experimental import pallas as pl
from jax.experimental.pallas import tpu as pltpu

DEPTH = 4
N_MIXERS = 3
DEEPNORM_ALPHA = (2.0 * DEPTH) ** 0.25
LN_EPS = 1e-5

A_CHUNK = 128
A_GROUPS = 8
B_GROUPS = 4
POOL_WINDOWS = (2, 4, 8, 16)
POOL_ROWS = 256
LANES = 128
POOL_HALO = 16
POOL_QUAD = 4

V7X_VMEM_LIMIT_BYTES = 56 * 1024 * 1024
POOL_TILE = 1024
GMLP_TILE = 1024
GMLP_ROWS = 256
GMLP_COLS = 256
FFN_TILE = 1024
FFN_ROW_BLOCKS = (256, 256, 256, 256)
FFN_CHUNK = 1024
DFT_RADIX = 4
DFT_TILE = 256
DFT_QSTEP = 2
FNET_IN_RES = 2
FNET_IN_TILE = 256

BF16 = jnp.bfloat16
F32 = jnp.float32


def _ln(r, g, b):
    mu = jnp.mean(r, axis=-1, keepdims=True)
    c = r - mu
    var = jnp.mean(c * c, axis=-1, keepdims=True)
    return c * lax.rsqrt(var + LN_EPS) * g + b


def _gelu_tanh(x):
    c = math.sqrt(2.0 / math.pi)
    half = 0.5 * x
    return half + half * jnp.tanh(x * (c + (c * 0.044715) * (x * x)))


def _dot(a, b):
    return jnp.dot(a, b, preferred_element_type=F32)


def _const_spec(shape):
    zeros = (0,) * len(shape)
    return pl.BlockSpec(shape, lambda *_: zeros, pipeline_mode=pl.Buffered(1))


def _layer_spec(shape, layer):
    zeros = (0,) * len(shape)
    return pl.BlockSpec((None,) + tuple(shape), lambda *_: (layer,) + zeros,
                        pipeline_mode=pl.Buffered(1))


def _params(n_axes):
    return pltpu.CompilerParams(
        dimension_semantics=("arbitrary",) * n_axes,
        vmem_limit_bytes=V7X_VMEM_LIMIT_BYTES)


def _ffn_kernel(x_ref, w1_ref, b1_ref, w2_ref, b2_ref, g_ref, b_ref, o_ref):
    d_ff = w1_ref.shape[1]
    n_chunks = d_ff // FFN_CHUNK
    x = x_ref[...]
    xb = x.astype(BF16)
    acc = jnp.zeros(x.shape, F32)
    for c in range(n_chunks):
        cols = slice(c * FFN_CHUNK, (c + 1) * FFN_CHUNK)
        h = _dot(xb, w1_ref[:, cols]) + b1_ref[:, cols]
        h = jnp.square(jnp.maximum(h, 0.0)).astype(BF16)
        if c < n_chunks - 1:
            acc = acc + _dot(h, w2_ref[cols, :])
            continue
        row0 = 0
        for n_rows in FFN_ROW_BLOCKS:
            rows = slice(row0, row0 + n_rows)
            row0 += n_rows
            y = acc[rows] + _dot(h[rows], w2_ref[cols, :]) + b2_ref[...]
            o_ref[rows, :] = _ln(DEEPNORM_ALPHA * x[rows] + y, g_ref[...],
                                 b_ref[...])


def _ffn(x2d, w1_all, b1, w2_all, b2, g, b, layer):
    n, d = x2d.shape
    f = w1_all.shape[2]
    row = pl.BlockSpec((FFN_TILE, d), lambda i: (i, 0))
    return pl.pallas_call(
        _ffn_kernel,
        grid=(n // FFN_TILE,),
        in_specs=[row, _layer_spec((d, f), layer), _const_spec((1, f)),
                  _layer_spec((f, d), layer), _const_spec((1, d)),
                  _const_spec((1, d)), _const_spec((1, d))],
        out_specs=row,
        out_shape=jax.ShapeDtypeStruct((n, d), F32),
        compiler_params=_params(1),
        name="ffn",
    )(x2d, w1_all, b1.reshape(1, f), w2_all,
      b2.reshape(1, d), g.reshape(1, d), b.reshape(1, d))


def _gmlp_kernel(x_ref, w_in_ref, lng_ref, lnb_ref, ws_ref, bs_ref, w_out_ref,
                 g_ref, b_ref, o_ref):
    x = x_ref[...]
    xb = x.astype(BF16)
    width = w_out_ref.shape[0]
    gd = width // A_GROUPS
    n_chunks = x.shape[0] // A_CHUNK

    def proj(col0):
        return jnp.concatenate(
            [_gelu_tanh(_dot(xb, w_in_ref[:, c:c + GMLP_COLS]))
             for c in range(col0, col0 + width, GMLP_COLS)], axis=1)

    v = _ln(proj(width), lng_ref[...], lnb_ref[...]).astype(BF16)
    u = proj(0)
    mixed_groups = []
    for gi in range(A_GROUPS):
        cols = slice(gi * gd, (gi + 1) * gd)
        v_g = jnp.concatenate(
            [v[n * A_CHUNK:(n + 1) * A_CHUNK, cols] for n in range(n_chunks)],
            axis=1)
        mixed_groups.append(_dot(ws_ref[gi], v_g))
    per_block = GMLP_ROWS // A_CHUNK
    for blk in range(n_chunks // per_block):
        rows = slice(blk * GMLP_ROWS, (blk + 1) * GMLP_ROWS)
        mixed = jnp.concatenate(
            [jnp.concatenate(
                [m[:, n * gd:(n + 1) * gd] for m in mixed_groups], axis=1)
             + bs_ref[...]
             for n in range(blk * per_block, (blk + 1) * per_block)], axis=0)
        out = u[rows] * mixed
        y = _dot(out.astype(BF16), w_out_ref[...])
        o_ref[rows, :] = _ln(DEEPNORM_ALPHA * x[rows] + y, g_ref[...], b_ref[...])


def _gmlp_mixer(x2d, w_in_all, ln_g, ln_b, w_s_all, b_s, w_out_all, g, b, layer):
    n, d = x2d.shape
    width = w_out_all.shape[1]
    gd = width // A_GROUPS
    bias_map = jnp.repeat(b_s.T, gd, axis=1)
    row = pl.BlockSpec((GMLP_TILE, d), lambda i: (i, 0))
    return pl.pallas_call(
        _gmlp_kernel,
        grid=(n // GMLP_TILE,),
        in_specs=[row, _layer_spec((d, 2 * width), layer),
                  _const_spec((1, width)), _const_spec((1, width)),
                  _layer_spec((A_GROUPS, A_CHUNK, A_CHUNK), layer),
                  _const_spec((A_CHUNK, width)),
                  _layer_spec((width, d), layer),
                  _const_spec((1, d)), _const_spec((1, d))],
        out_specs=row,
        out_shape=jax.ShapeDtypeStruct((n, d), F32),
        compiler_params=_params(1),
        name="gmlp_mixer",
    )(x2d, w_in_all, ln_g.reshape(1, width), ln_b.reshape(1, width),
      w_s_all, bias_map, w_out_all, g.reshape(1, d), b.reshape(1, d))


def _fnet_in_kernel(x_ref, w_in_ref, lng_ref, lnb_ref, z_ref, x_scr):
    d = w_in_ref.shape[0]
    rows = x_ref.shape[0] // DFT_RADIX
    n_lane_blocks = d // LANES
    for c in range(n_lane_blocks):
        x_scr[c] = x_ref[:, c * LANES:(c + 1) * LANES]
    gd = d // B_GROUPS
    for r0 in range(0, DFT_RADIX, FNET_IN_RES):
        res = range(r0, r0 + FNET_IN_RES)
        xs = jnp.concatenate(
            [jnp.concatenate(
                [x_scr[pl.ds(c, 1), pl.ds(r, rows, stride=DFT_RADIX), :][0]
                 for c in range(n_lane_blocks)], axis=1)
             for r in res], axis=0)
        z = _dot(xs.astype(BF16), w_in_ref[...])
        for gi in range(B_GROUPS):
            cols = slice(gi * gd, (gi + 1) * gd)
            zn = _ln(z[:, cols], lng_ref[:, cols], lnb_ref[:, cols]).astype(BF16)
            for k, r in enumerate(res):
                z_ref[0, r, :, cols] = zn[k * rows:(k + 1) * rows]


def _fnet_in(x, w_in, ln_g, ln_b):
    bsz, s, d = x.shape
    sq = s // DFT_RADIX
    return pl.pallas_call(
        _fnet_in_kernel,
        grid=(bsz, sq // FNET_IN_TILE),
        in_specs=[pl.BlockSpec((None, DFT_RADIX * FNET_IN_TILE, d),
                               lambda bi, ti: (bi, ti, 0)),
                  _const_spec((d, d)), _const_spec((1, d)), _const_spec((1, d))],
        out_specs=pl.BlockSpec((1, DFT_RADIX, FNET_IN_TILE, d),
                               lambda bi, ti: (bi, 0, ti, 0)),
        out_shape=jax.ShapeDtypeStruct((bsz, DFT_RADIX, sq, d), BF16),
        scratch_shapes=[pltpu.VMEM(
            (d // LANES, DFT_RADIX * FNET_IN_TILE, LANES), F32)],
        compiler_params=_params(2),
        name="fnet_in",
    )(x, w_in.astype(BF16), ln_g.reshape(1, d), ln_b.reshape(1, d))


def _fnet_dft_kernel(x_ref, z_ref, tc_ref, ts_ref, cdft_ref, w_out_ref,
                     g_ref, b_ref, o_ref, p_scr, q_scr):
    step = pl.program_id(0)
    per_macro = DFT_RADIX // DFT_QSTEP
    n_macro = (pl.num_programs(0) - 1) // per_macro
    mi = jnp.maximum(step - 1, 0) // per_macro
    qi = jnp.maximum(step - 1, 0) % per_macro
    rows = DFT_QSTEP * DFT_TILE

    def finish():
        slot = mi % 2
        quads = pl.ds(qi * DFT_QSTEP, DFT_QSTEP)
        x = x_ref[0].reshape(rows, x_ref.shape[-1])
        p = p_scr[slot, quads].reshape(rows, p_scr.shape[-1])
        q = q_scr[slot, quads].reshape(rows, q_scr.shape[-1])
        gd = p.shape[1] // B_GROUPS
        f_groups = []
        for gi in range(B_GROUPS):
            cols = slice(gi * gd, (gi + 1) * gd)
            pq = jnp.concatenate([p[:, cols], q[:, cols]], axis=1)
            f_groups.append(_dot(pq, cdft_ref[...]))
        f = jnp.concatenate(f_groups, axis=1).astype(BF16)
        for j in range(DFT_QSTEP):
            blk = slice(j * DFT_TILE, (j + 1) * DFT_TILE)
            y = _dot(f[blk], w_out_ref[...])
            o_ref[0, j] = _ln(DEEPNORM_ALPHA * x[blk] + y, g_ref[...], b_ref[...])

    starts_macro = jnp.where(step == 0, 0, mi + 1)

    def transform():
        slot = starts_macro % 2
        tc = {r: _dot(tc_ref[r], z_ref[0, r]) for r in (0, 2, 1, 3)}
        ts = {r: _dot(ts_ref[r], z_ref[0, r]) for r in (0, 2, 1, 3)}
        c_ev_p, c_ev_m = tc[0] + tc[2], tc[0] - tc[2]
        c_od_p, c_od_m = tc[1] + tc[3], tc[1] - tc[3]
        s_ev_p, s_ev_m = ts[0] + ts[2], ts[0] - ts[2]
        s_od_p, s_od_m = ts[1] + ts[3], ts[1] - ts[3]
        p_scr[slot, 0] = (c_ev_p + c_od_p).astype(BF16)
        q_scr[slot, 0] = (s_ev_p + s_od_p).astype(BF16)
        p_scr[slot, 1] = (c_ev_m - s_od_m).astype(BF16)
        q_scr[slot, 1] = (s_ev_m + c_od_m).astype(BF16)
        p_scr[slot, 2] = (c_ev_p - c_od_p).astype(BF16)
        q_scr[slot, 2] = (s_ev_p - s_od_p).astype(BF16)
        p_scr[slot, 3] = (c_ev_m + s_od_m).astype(BF16)
        q_scr[slot, 3] = (s_ev_m - c_od_m).astype(BF16)

    transforms = (qi == per_macro - 1) & (mi + 1 < n_macro)

    @pl.when(step == 0)
    def _():
        transform()

    @pl.when((step > 0) & transforms)
    def _():
        finish()
        transform()

    @pl.when((step > 0) & jnp.logical_not(transforms))
    def _():
        finish()


def _cos_sin(index, period):
    ang = (index % period).astype(np.float64) * (2.0 * math.pi / period)
    return np.cos(ang), np.sin(ang)


def _position_dft_tables(s):
    sq = s // DFT_RADIX
    k = np.arange(sq, dtype=np.int64)[None, :, None]
    sp = np.arange(sq, dtype=np.int64)[None, None, :]
    r = np.arange(DFT_RADIX, dtype=np.int64)[:, None, None]
    return _cos_sin(k * (DFT_RADIX * sp + r), s)


def _channel_dft_table(gd, norm):
    j = np.arange(gd, dtype=np.int64)
    cos_c, sin_c = _cos_sin(j[:, None] * j[None, :], gd)
    return np.concatenate([cos_c, -sin_c], axis=0) * norm


def _fnet_dft(x, z, w_out, g, b):
    bsz, s, d = x.shape
    gd = d // B_GROUPS
    sq = s // DFT_RADIX
    tc, ts = _position_dft_tables(s)
    cdft = _channel_dft_table(gd, 1.0 / math.sqrt(s * gd))
    x4 = x.reshape(bsz, DFT_RADIX, sq, d)
    n_kt = sq // DFT_TILE
    n_macro = bsz * n_kt

    per_macro = DFT_RADIX // DFT_QSTEP

    def finishing(step):
        return jnp.maximum(step - 1, 0) // per_macro

    def transforming(step):
        return jnp.where(step == 0, 0,
                         jnp.minimum(finishing(step) + 1, n_macro - 1))

    tile = pl.BlockSpec(
        (1, DFT_QSTEP, DFT_TILE, d),
        lambda step: (finishing(step) // n_kt,
                      jnp.maximum(step - 1, 0) % per_macro,
                      finishing(step) % n_kt, 0))
    seq = pl.BlockSpec((1, DFT_RADIX, sq, d),
                       lambda step: (transforming(step) // n_kt, 0, 0, 0))
    table = pl.BlockSpec((DFT_RADIX, DFT_TILE, sq),
                         lambda step: (0, transforming(step) % n_kt, 0))
    out = pl.pallas_call(
        _fnet_dft_kernel,
        grid=(1 + n_macro * per_macro,),
        in_specs=[tile, seq, table, table, _const_spec((2 * gd, gd)),
                  _const_spec((d, d)), _const_spec((1, d)), _const_spec((1, d))],
        out_specs=tile,
        out_shape=jax.ShapeDtypeStruct((bsz, DFT_RADIX, sq, d), F32),
        scratch_shapes=[pltpu.VMEM((2, DFT_RADIX, DFT_TILE, d), BF16),
                        pltpu.VMEM((2, DFT_RADIX, DFT_TILE, d), BF16)],
        compiler_params=_params(1),
        name="fnet_dft",
    )(x4, z, jnp.asarray(tc, BF16), jnp.asarray(ts, BF16),
      jnp.asarray(cdft, BF16), w_out.astype(BF16),
      g.reshape(1, d), b.reshape(1, d))
    return out.reshape(bsz, s, d)


def _fnet_mixer(x, w_in, ln_g, ln_b, w_out, g, b):
    d = x.shape[-1]
    z = _fnet_in(x, w_in, ln_g.reshape(d), ln_b.reshape(d))
    return _fnet_dft(x, z, w_out, g, b)


def _pool_kernel(x_ref, prev_ref, next_ref, w_in_ref, w_grp_ref, scale_ref,
                 w_out_ref, g_ref, b_ref, o_ref, z_scr, quad_scr, *, seq_len):
    ti = pl.program_id(1)
    tm = x_ref.shape[1]
    x = x_ref[0]
    first = ti == 0
    last = ti == pl.num_programs(1) - 1
    x_prev = jnp.where(first, 0.0, prev_ref[0])
    x_next = jnp.where(last, 0.0, next_ref[0])
    x_all = jnp.concatenate([x_prev, x, x_next], axis=0)
    z_scr[...] = _dot(x_all.astype(BF16), w_in_ref[...])

    t = ti * tm + lax.broadcasted_iota(jnp.int32, (tm, 1), 0)
    gd = x.shape[1] // len(POOL_WINDOWS)
    wide = [gi for gi, w in enumerate(POOL_WINDOWS) if w >= 2 * POOL_QUAD]
    wide_col0 = wide[0] * gd
    max_half = max(POOL_WINDOWS) // 2
    quad_row0 = POOL_HALO - max_half
    quad_rows = tm + 2 * max_half
    quad = z_scr[quad_row0:quad_row0 + quad_rows, wide_col0:]
    for k in range(1, POOL_QUAD):
        quad = quad + z_scr[quad_row0 + k:quad_row0 + k + quad_rows, wide_col0:]
    quad_scr[quad_row0:quad_row0 + quad_rows, :] = quad

    mixed_groups = []
    for gi, w in enumerate(POOL_WINDOWS):
        cols = slice(gi * gd, (gi + 1) * gd)
        half = w // 2
        row0 = POOL_HALO - half
        if gi in wide:
            qcols = slice(gi * gd - wide_col0, (gi + 1) * gd - wide_col0)
            win = quad_scr[row0:row0 + tm, qcols]
            for m in range(1, w // POOL_QUAD):
                r = row0 + m * POOL_QUAD
                win = win + quad_scr[r:r + tm, qcols]
        else:
            win = z_scr[row0:row0 + tm, cols]
            for k in range(1, w):
                win = win + z_scr[row0 + k:row0 + k + tm, cols]
        lo = jnp.maximum(t - half, 0)
        hi = jnp.minimum(t - half + w, seq_len)
        inv_count = 1.0 / (hi - lo).astype(F32)
        pooled = win * inv_count - z_scr[POOL_HALO:POOL_HALO + tm, cols]
        mixed_g = _dot(pooled.astype(BF16), w_grp_ref[gi]) * scale_ref[:, cols]
        mixed_groups.append(mixed_g.astype(BF16))
    mixed = jnp.concatenate(mixed_groups, axis=1)
    for rb in range(tm // POOL_ROWS):
        rows = slice(rb * POOL_ROWS, (rb + 1) * POOL_ROWS)
        y = _dot(mixed[rows], w_out_ref[...])
        o_ref[0, rows, :] = _ln(DEEPNORM_ALPHA * x[rows] + y, g_ref[...],
                                b_ref[...])


def _pool_mixer(x, w_in, w_grp, scale, w_out, g, b):
    bsz, s, d = x.shape
    n_g = len(POOL_WINDOWS)
    gd = d // n_g
    tm = POOL_TILE
    n_narrow = sum(w < 2 * POOL_QUAD for w in POOL_WINDOWS)
    per_tile = tm // POOL_HALO
    n_halo = s // POOL_HALO
    tile = pl.BlockSpec((1, tm, d), lambda bi, ti: (bi, ti, 0))
    prev = pl.BlockSpec(
        (1, POOL_HALO, d),
        lambda bi, ti: (bi, jnp.maximum(ti * per_tile - 1, 0), 0))
    nxt = pl.BlockSpec(
        (1, POOL_HALO, d),
        lambda bi, ti: (bi, jnp.minimum((ti + 1) * per_tile, n_halo - 1), 0))
    return pl.pallas_call(
        functools.partial(_pool_kernel, seq_len=s),
        grid=(bsz, s // tm),
        in_specs=[tile, prev, nxt, _const_spec((d, d)),
                  _const_spec((n_g, gd, gd)), _const_spec((1, d)),
                  _const_spec((d, d)), _const_spec((1, d)), _const_spec((1, d))],
        out_specs=tile,
        out_shape=jax.ShapeDtypeStruct((bsz, s, d), F32),
        scratch_shapes=[
            pltpu.VMEM((tm + 2 * POOL_HALO, d), F32),
            pltpu.VMEM((tm + 2 * POOL_HALO, d - n_narrow * gd), F32)],
        compiler_params=_params(2),
        name="pool_mixer",
    )(x, x, x, w_in.astype(BF16), w_grp.astype(BF16), scale.reshape(1, d),
      w_out.astype(BF16), g.reshape(1, d), b.reshape(1, d))


def kernel(x, ln1_g, ln1_b, ffn_w1, ffn_b1, ffn_w2, ffn_b2, ln2_g, ln2_b,
           a_w_in, a_ln_g, a_ln_b, a_w_s, a_b_s, a_w_out,
           b_w_in, b_ln_g, b_ln_b, b_w_out,
           c_w_in, c_w_grp, c_scale, c_w_out):
    bsz, s, d = x.shape
    ffn_w1_b, ffn_w2_b = ffn_w1.astype(BF16), ffn_w2.astype(BF16)
    a_w_in_b, a_w_s_b = a_w_in.astype(BF16), a_w_s.astype(BF16)
    a_w_out_b = a_w_out.astype(BF16)
    for i in range(DEPTH):
        kind, j = i % N_MIXERS, i // N_MIXERS
        if kind == 0:
            x = _gmlp_mixer(x.reshape(bsz * s, d), a_w_in_b, a_ln_g[j],
                            a_ln_b[j], a_w_s_b, a_b_s[j], a_w_out_b,
                            ln1_g[i], ln1_b[i], j)
        elif kind == 1:
            x = _fnet_mixer(x.reshape(bsz, s, d), b_w_in[j], b_ln_g[j],
                            b_ln_b[j], b_w_out[j], ln1_g[i], ln1_b[i])
        else:
            x = _pool_mixer(x.reshape(bsz, s, d), c_w_in[j], c_w_grp[j],
                            c_scale[j], c_w_out[j], ln1_g[i], ln1_b[i])
        x = _ffn(x.reshape(bsz * s, d), ffn_w1_b, ffn_b1[i], ffn_w2_b,
                 ffn_b2[i], ln2_g[i], ln2_b[i], i)
    return x.reshape(bsz, s, d)
```
